```python
import jax, jax.numpy as jnp
from jax import lax
import numpy as np

D_MODEL = 1024
BATCH = 8
SEQ = 2048
DEPTH = 1

A_HEADS = 8
A_HEAD_DIM = 128
A_WIDTH = A_HEADS * A_HEAD_DIM
IDX_HEADS = 8
IDX_DIM = 64
TOPK_MAX = 256
Q_BLOCK = 64
B_HEADS = 8
B_KEY_DIM = 128
B_VAL_DIM = 128
B_QK_WIDTH = B_HEADS * B_KEY_DIM
B_V_WIDTH = B_HEADS * B_VAL_DIM
CONV_WIDTH = 4
CHUNK = 64
ROPE_THETA = 500000.0
ROPE_FRACTION = 4
EPS = 1e-6
SPLIT_SIZES = (A_WIDTH, A_WIDTH, A_WIDTH, A_WIDTH,
               IDX_HEADS * IDX_DIM, IDX_DIM, IDX_HEADS,
               B_QK_WIDTH, B_QK_WIDTH, B_V_WIDTH, B_V_WIDTH, B_HEADS, B_HEADS,
               D_MODEL, D_MODEL)
D_IN = 4 * A_WIDTH + IDX_HEADS * IDX_DIM + IDX_DIM + IDX_HEADS + 2 * B_QK_WIDTH + 2 * B_V_WIDTH + 2 * B_HEADS + 2 * D_MODEL

kernel_name = "hybrid_dsa_gdn_gated_merge_block"


def rmsnorm(x, gain):
    xf = x.astype(jnp.float32)
    y = xf * lax.rsqrt(jnp.mean(xf * xf, axis=-1, keepdims=True) + EPS)
    return (y * gain.astype(jnp.float32)).astype(x.dtype)


def layernorm(x, gain, bias):
    xf = x.astype(jnp.float32)
    mu = jnp.mean(xf, axis=-1, keepdims=True)
    var = jnp.mean(jnp.square(xf - mu), axis=-1, keepdims=True)
    y = (xf - mu) * lax.rsqrt(var + EPS)
    return (y * gain.astype(jnp.float32) + bias.astype(jnp.float32)).astype(x.dtype)


def l2norm(x):
    xf = x.astype(jnp.float32)
    return xf * lax.rsqrt(jnp.sum(xf * xf, axis=-1, keepdims=True) + EPS)


def partial_rope(x, positions):
    d = x.shape[-1]
    rot = d // ROPE_FRACTION
    half = rot // 2
    inv_freq = ROPE_THETA ** (-(jnp.arange(half, dtype=jnp.float32) * 2.0 / rot))
    ang = positions.astype(jnp.float32)[..., None] * inv_freq
    cos = jnp.cos(ang)[:, :, None, :].astype(x.dtype)
    sin = jnp.sin(ang)[:, :, None, :].astype(x.dtype)
    x1, x2, rest = x[..., :half], x[..., half:rot], x[..., rot:]
    return jnp.concatenate([x1 * cos - x2 * sin, x2 * cos + x1 * sin, rest], axis=-1)


def dsa_attention(q, k, v, q_idx, k_idx, w_idx):
    b, s, h, dh = q.shape
    topk = min(TOPK_MAX, s // 4)
    nb = s // Q_BLOCK

    def to_blocks(t):
        return jnp.swapaxes(t.reshape((b, nb, Q_BLOCK) + t.shape[2:]), 0, 1)

    tpos = jnp.arange(s, dtype=jnp.int32).reshape(nb, Q_BLOCK)
    key_pos = jnp.arange(s, dtype=jnp.int32)
    gather = jax.vmap(lambda kk, ii: kk[ii])

    def block(args):
        qb, qib, wb, tb = args
        logits = jnp.einsum('bqhd,bsd->bqhs', qib, k_idx) * (IDX_DIM ** -0.5)
        score = jnp.einsum('bqh,bqhs->bqs', wb, jax.nn.relu(logits)) * (IDX_HEADS ** -0.5)
        causal = key_pos[None, :] <= tb[:, None]
        score = jnp.where(causal[None], score.astype(jnp.float32), -jnp.inf)
        _, idx = lax.top_k(score, topk)
        valid = idx <= tb[None, :, None]
        k_sel = gather(k, idx)
        v_sel = gather(v, idx)
        att = jnp.einsum('bqhd,bqkhd->bhqk', qb, k_sel).astype(jnp.float32) * (dh ** -0.5)
        att = jnp.where(valid[:, None], att, -jnp.inf)
        p = jax.nn.softmax(att, axis=-1).astype(v.dtype)
        return jnp.einsum('bhqk,bqkhd->bqhd', p, v_sel)

    out = lax.map(block, (to_blocks(q), to_blocks(q_idx), to_blocks(w_idx), tpos))
    return jnp.swapaxes(out, 0, 1).reshape(b, s, h, dh)


def causal_depthwise_conv(x, w):
    ch = x.shape[-1]
    return lax.conv_general_dilated(x, w[:, None, :].astype(x.dtype), window_strides=(1,),
                                    padding=[(CONV_WIDTH - 1, 0)],
                                    dimension_numbers=('NWC', 'WIO', 'NWC'),
                                    feature_group_count=ch)


def gated_delta_rule_chunked(q, k, v, g, beta):
    b, s, h, dk = q.shape
    dv = v.shape[-1]
    n = s // CHUNK
    f32 = jnp.float32

    def chunks(t):
        t = jnp.moveaxis(t.astype(f32), 2, 1)
        return t.reshape((b, h, n, CHUNK) + t.shape[3:])

    qc, kc, vc, gch, bc = chunks(q), chunks(k), chunks(v), chunks(g), chunks(beta)
    gc = jnp.cumsum(gch, axis=-1)
    ii = jnp.arange(CHUNK)
    incl = ii[:, None] >= ii[None, :]
    strict = ii[:, None] > ii[None, :]
    decay = jnp.exp(jnp.where(incl, gc[..., :, None] - gc[..., None, :], -jnp.inf))
    kb = kc * bc[..., None]
    vb = vc * bc[..., None]
    lower = jnp.where(strict, jnp.einsum('bhncd,bhnjd->bhncj', kb, kc) * decay, 0.0)
    eye = jnp.eye(CHUNK, dtype=f32)
    t_mat = lax.linalg.triangular_solve(lower + eye, jnp.broadcast_to(eye, lower.shape),
                                        left_side=True, lower=True, unit_diagonal=True)
    u = jnp.einsum('bhncj,bhnjv->bhncv', t_mat, vb)
    w = jnp.einsum('bhncj,bhnjd->bhncd', t_mat, kb * jnp.exp(gc)[..., None])
    intra = jnp.einsum('bhncd,bhnjd->bhncj', qc, kc) * decay

    def step(state, xs):
        q_n, k_n, u_n, w_n, g_n, a_n = xs
        v_new = u_n - jnp.einsum('bhcd,bhdv->bhcv', w_n, state)
        o = (jnp.einsum('bhcd,bhdv->bhcv', q_n * jnp.exp(g_n)[..., None], state)
             + jnp.einsum('bhcj,bhjv->bhcv', a_n, v_new))
        g_last = g_n[..., -1]
        state = (state * jnp.exp(g_last)[..., None, None]
                 + jnp.einsum('bhcd,bhcv->bhdv', k_n * jnp.exp(g_last[..., None] - g_n)[..., None], v_new))
        return state, o

    xs = tuple(jnp.moveaxis(t, 2, 0) for t in (qc, kc, u, w, gc, intra))
    state0 = jnp.zeros((b, h, dk, dv), f32)
    _, o = lax.scan(step, state0, xs)
    o = jnp.moveaxis(o, 0, 2).reshape(b, h, s, dv)
    return jnp.moveaxis(o, 1, 2).astype(v.dtype)


def hybrid_layer(x, c, positions, w_ada, b_ada, pre_norm_gain, post_norm_gain, w_in,
                 idx_k_ln_gain, idx_k_ln_bias, dn_conv_w, dn_a_log, dn_dt_bias, dn_norm_gain,
                 w_a_out, w_b_out, w_o):
    b, s, _ = x.shape
    mod = c @ w_ada + b_ada
    shift, scale, gate = jnp.split(mod, 3, axis=-1)
    h = rmsnorm(x, pre_norm_gain) * (1.0 + scale[:, None, :]) + shift[:, None, :]
    proj = h @ w_in
    points = list(np.cumsum(SPLIT_SIZES)[:-1])
    (aq, ak, av, az, iq, ik, iw, bq, bk, bv, bz, bbeta, ba, ga, gb) = jnp.split(proj, points, axis=-1)

    qa = partial_rope(aq.reshape(b, s, A_HEADS, A_HEAD_DIM), positions)
    ka = partial_rope(ak.reshape(b, s, A_HEADS, A_HEAD_DIM), positions)
    va = av.reshape(b, s, A_HEADS, A_HEAD_DIM)
    q_idx = partial_rope(iq.reshape(b, s, IDX_HEADS, IDX_DIM), positions)
    k_idx = partial_rope(layernorm(ik, idx_k_ln_gain, idx_k_ln_bias)[:, :, None, :], positions)[:, :, 0, :]
    o_a = dsa_attention(qa, ka, va, q_idx, k_idx, iw).reshape(b, s, A_WIDTH)
    y_a = (o_a * jax.nn.silu(az)) @ w_a_out

    qkv = jax.nn.silu(causal_depthwise_conv(jnp.concatenate([bq, bk, bv], axis=-1), dn_conv_w))
    cq, ck, cv = jnp.split(qkv, [B_QK_WIDTH, 2 * B_QK_WIDTH], axis=-1)
    qb = l2norm(cq.reshape(b, s, B_HEADS, B_KEY_DIM)) * (B_KEY_DIM ** -0.5)
    kb = l2norm(ck.reshape(b, s, B_HEADS, B_KEY_DIM))
    vb = cv.reshape(b, s, B_HEADS, B_VAL_DIM)
    beta = jax.nn.sigmoid(bbeta.astype(jnp.float32))
    g = -jnp.exp(dn_a_log.astype(jnp.float32)) * jax.nn.softplus(ba.astype(jnp.float32) + dn_dt_bias.astype(jnp.float32))
    o_b = gated_delta_rule_chunked(qb, kb, vb, g, beta)
    o_b = rmsnorm(o_b, dn_norm_gain) * jax.nn.silu(bz.reshape(b, s, B_HEADS, B_VAL_DIM))
    y_b = o_b.reshape(b, s, B_V_WIDTH) @ w_b_out

    y = jax.nn.sigmoid(ga) * y_a + jax.nn.sigmoid(gb) * y_b
    out = y @ w_o
    return x + gate[:, None, :] * rmsnorm(out, post_norm_gain)


def setup_inputs(seed: int = 0) -> dict:
    key = jax.random.key(seed)
    ks = jax.random.split(key, 20)
    nrm = jax.random.normal
    d = D_MODEL
    conv_ch = 2 * B_QK_WIDTH + B_V_WIDTH
    dt = jnp.exp(jax.random.uniform(ks[10], (DEPTH, B_HEADS), minval=np.log(1e-3), maxval=np.log(1e-1)))
    return {
        "x": nrm(ks[0], (BATCH, SEQ, d), jnp.float32),
        "c": nrm(ks[1], (BATCH, d), jnp.float32),
        "positions": (jnp.arange(SEQ, dtype=jnp.int32)[None, :]
                      + jax.random.randint(ks[2], (BATCH, 1), 0, 1024, dtype=jnp.int32)),
        "w_ada": nrm(ks[3], (DEPTH, d, 3 * d), jnp.float32) * (0.5 * d ** -0.5),
        "b_ada": nrm(ks[4], (DEPTH, 3 * d), jnp.float32) * 0.01,
        "pre_norm_gain": 1.0 + 0.05 * nrm(ks[5], (DEPTH, d), jnp.float32),
        "post_norm_gain": 1.0 + 0.05 * nrm(ks[6], (DEPTH, d), jnp.float32),
        "w_in": nrm(ks[7], (DEPTH, d, D_IN), jnp.float32) * d ** -0.5,
        "idx_k_ln_gain": 1.0 + 0.05 * nrm(ks[8], (DEPTH, IDX_DIM), jnp.float32),
        "idx_k_ln_bias": 0.02 * nrm(ks[9], (DEPTH, IDX_DIM), jnp.float32),
        "dn_conv_w": nrm(ks[11], (DEPTH, CONV_WIDTH, conv_ch), jnp.float32) * CONV_WIDTH ** -0.5,
        "dn_a_log": jnp.log(jax.random.uniform(ks[12], (DEPTH, B_HEADS), minval=1.0, maxval=16.0)),
        "dn_dt_bias": dt + jnp.log(-jnp.expm1(-dt)),
        "dn_norm_gain": 1.0 + 0.05 * nrm(ks[13], (DEPTH, B_VAL_DIM), jnp.float32),
        "w_a_out": nrm(ks[14], (DEPTH, A_WIDTH, d), jnp.float32) * A_WIDTH ** -0.5,
        "w_b_out": nrm(ks[15], (DEPTH, B_V_WIDTH, d), jnp.float32) * B_V_WIDTH ** -0.5,
        "w_o": nrm(ks[16], (DEPTH, d, d), jnp.float32) * d ** -0.5,
    }


def reference(x, c, positions, w_ada, b_ada, pre_norm_gain, post_norm_gain, w_in,
              idx_k_ln_gain, idx_k_ln_bias, dn_conv_w, dn_a_log, dn_dt_bias, dn_norm_gain,
              w_a_out, w_b_out, w_o):
    for layer in range(DEPTH):
        x = hybrid_layer(x, c, positions, w_ada[layer], b_ada[layer], pre_norm_gain[layer],
                         post_norm_gain[layer], w_in[layer], idx_k_ln_gain[layer], idx_k_ln_bias[layer],
                         dn_conv_w[layer], dn_a_log[layer], dn_dt_bias[layer], dn_norm_gain[layer],
                         w_a_out[layer], w_b_out[layer], w_o[layer])
    return x
```

```python
import functools

import numpy as np
import jax
import jax.numpy as jnp
from jax import lax
from jax.experimental import pallas as pl
from jax.experimental.pallas import tpu as pltpu

D_MODEL = 1024
N_HEADS = 8
HEAD_DIM = 128
IDX_HEADS = 8
IDX_DIM = 64
TOPK_MAX = 256
CONV_WIDTH = 4
CHUNK = 64
ROPE_THETA = 500000.0
ROPE_FRACTION = 4
EPS = 1e-6

LANES = 128
MXU_DIM = 256
TQ = MXU_DIM
GROUP = MXU_DIM
SMALL_W = 768
NEG_BIG = -1e30
INT_MIN = -2147483648

CB_AQ, CB_AK, CB_AZ, CB_BQ, CB_BK, CB_BV, CB_BZ, CB_GA, CB_GB = range(9)
N_BIG = 9 * D_MODEL

_F32 = jnp.float32
_BF16 = jnp.bfloat16


def _cparams(sem, vmem_mb):
    return pltpu.CompilerParams(dimension_semantics=sem, vmem_limit_bytes=vmem_mb * 1024 * 1024)


def _split2(a):
    hi = a.astype(_BF16)
    lo = (a - hi.astype(_F32)).astype(_BF16)
    return hi, lo


def _dot(a, b):
    return jnp.dot(a, b, preferred_element_type=_F32)


def _dot_nt(a, b):
    return lax.dot_general(a, b, (((1,), (1,)), ((), ())), preferred_element_type=_F32)


def _silu(x):
    return x * jax.nn.sigmoid(x)


def _ada_kernel(c_ref, w_ref, b_ref, o_ref):
    c1, c2 = _split2(c_ref[...])
    c3 = (c_ref[...] - c1.astype(_F32) - c2.astype(_F32)).astype(_BF16)
    w = w_ref[...]
    w1, w2 = _split2(w)
    w3 = (w - w1.astype(_F32) - w2.astype(_F32)).astype(_BF16)
    acc = _dot(c1, w3) + _dot(c2, w2) + _dot(c3, w1)
    acc = acc + _dot(c1, w2) + _dot(c2, w1)
    acc = acc + _dot(c1, w1)
    o_ref[...] = acc + b_ref[...]


def _ada(c, w_ada, b_ada):
    bsz = c.shape[0]
    n = w_ada.shape[1]
    tn = 512
    return pl.pallas_call(
        _ada_kernel,
        grid=(n // tn,),
        in_specs=[pl.BlockSpec((bsz, D_MODEL), lambda j: (0, 0)),
                  pl.BlockSpec((D_MODEL, tn), lambda j: (0, j)),
                  pl.BlockSpec((1, tn), lambda j: (0, j))],
        out_specs=pl.BlockSpec((bsz, tn), lambda j: (0, j)),
        out_shape=jax.ShapeDtypeStruct((bsz, n), _F32),
        compiler_params=_cparams(("arbitrary",), 32),
        name="ada",
    )(c, w_ada, b_ada.reshape(1, n))


def _proj_kernel(x_ref, mod_ref, gain_ref, wbig_ref, wsh_ref, wsl_ref, wvt_ref,
                 big_ref, small_ref, vt_ref, h_ref):
    j = pl.program_id(1)

    @pl.when(j == 0)
    def _():
        x = x_ref[...]
        ms = jnp.mean(x * x, axis=-1, keepdims=True)
        y = x * lax.rsqrt(ms + EPS) * gain_ref[...]
        shift = mod_ref[0, :, 0:D_MODEL]
        scale = mod_ref[0, :, D_MODEL:2 * D_MODEL]
        h = y * (1.0 + scale) + shift
        hi, lo = _split2(h)
        h_ref[...] = hi
        small_ref[...] = (_dot(hi, wsl_ref[...]) + _dot(lo, wsh_ref[...])) + _dot(hi, wsh_ref[...])
        vt = _dot_nt(wvt_ref[...], hi).astype(_BF16)
        for t in range(vt_ref.shape[0]):
            vt_ref[t] = vt[:, t * TQ:(t + 1) * TQ]

    big_ref[...] = _dot(h_ref[...], wbig_ref[...]).astype(_BF16)


def _proj(x2, mod3, pre_gain, w_big, ws_hi, ws_lo, w_vt, seq):
    m = x2.shape[0]
    tm, tn = 512, 1024
    per_b = seq // tm
    return pl.pallas_call(
        _proj_kernel,
        grid=(m // tm, N_BIG // tn),
        in_specs=[pl.BlockSpec((tm, D_MODEL), lambda i, j: (i, 0)),
                  pl.BlockSpec((1, 1, 3 * D_MODEL), lambda i, j: (i // per_b, 0, 0)),
                  pl.BlockSpec((1, D_MODEL), lambda i, j: (0, 0)),
                  pl.BlockSpec((D_MODEL, tn), lambda i, j: (0, j)),
                  pl.BlockSpec((D_MODEL, SMALL_W), lambda i, j: (0, 0)),
                  pl.BlockSpec((D_MODEL, SMALL_W), lambda i, j: (0, 0)),
                  pl.BlockSpec((D_MODEL, D_MODEL), lambda i, j: (0, 0))],
        out_specs=[pl.BlockSpec((tm, tn), lambda i, j: (i, j)),
                   pl.BlockSpec((tm, SMALL_W), lambda i, j: (i, 0)),
                   pl.BlockSpec((tm // TQ, D_MODEL, TQ), lambda i, j: (i, 0, 0))],
        out_shape=[jax.ShapeDtypeStruct((m, N_BIG), _BF16),
                   jax.ShapeDtypeStruct((m, SMALL_W), _F32),
                   jax.ShapeDtypeStruct((m // TQ, D_MODEL, TQ), _BF16)],
        scratch_shapes=[pltpu.VMEM((tm, D_MODEL), _BF16)],
        compiler_params=_cparams(("parallel", "arbitrary"), 48),
        name="proj",
    )(x2, mod3, pre_gain, w_big, ws_hi, ws_lo, w_vt)


def _rope(x, c, sn, sp, half):
    return x * c + pltpu.roll(x, LANES - half, 1) * sn + pltpu.roll(x, half, 1) * sp


def _prep_kernel(aq_ref, ak_ref, small_ref, pos_ref, fa_ref, fi_ref, lng_ref, lnb_ref,
                 alog_ref, dtb_ref, q_ref, k_ref, qcat_ref, kcat_ref, scal_ref):
    tr = aq_ref.shape[0]
    posf = pos_ref[...].astype(_F32)
    lane = lax.broadcasted_iota(jnp.int32, (1, LANES), 1)

    ang = posf * fa_ref[...]
    ca, sa = jnp.cos(ang), jnp.sin(ang)
    c_a = jnp.where(lane < 32, ca, 1.0)
    sn_a = jnp.where(lane < 16, -sa, 0.0)
    sp_a = jnp.where((lane >= 16) & (lane < 32), sa, 0.0)
    for h in range(N_HEADS):
        sl = slice(h * HEAD_DIM, (h + 1) * HEAD_DIM)
        xq = aq_ref[:, sl].astype(_F32)
        q_ref[:, sl] = (_rope(xq, c_a, sn_a, sp_a, 16) * (HEAD_DIM ** -0.5)).astype(_BF16)
        xk = ak_ref[:, sl].astype(_F32)
        k_ref[:, sl] = _rope(xk, c_a, sn_a, sp_a, 16).astype(_BF16)

    ang_i = posf * fi_ref[...]
    ci, si = jnp.cos(ang_i), jnp.sin(ang_i)
    l64 = lane % IDX_DIM
    c_i = jnp.where(l64 < 16, ci, 1.0)
    sn_i = jnp.where(l64 < 8, -si, 0.0)
    sp_i = jnp.where((l64 >= 8) & (l64 < 16), si, 0.0)
    first = lane < IDX_DIM
    for p in range(IDX_HEADS // 2):
        x = small_ref[:, p * LANES:(p + 1) * LANES]
        y = _rope(x, c_i, sn_i, sp_i, 8) * (IDX_DIM ** -0.5)
        hi = y.astype(_BF16).astype(_F32)
        lo = y - hi
        hi_sw = pltpu.roll(hi, IDX_DIM, 1)
        lo_sw = pltpu.roll(lo, IDX_DIM, 1)
        base = 2 * p * MXU_DIM
        qcat_ref[:, base:base + LANES] = jnp.where(first, hi, hi_sw).astype(_BF16)
        qcat_ref[:, base + LANES:base + 2 * LANES] = jnp.where(first, lo, lo_sw).astype(_BF16)
        qcat_ref[:, base + 2 * LANES:base + 3 * LANES] = jnp.where(first, hi_sw, hi).astype(_BF16)
        qcat_ref[:, base + 3 * LANES:base + 4 * LANES] = jnp.where(first, lo_sw, lo).astype(_BF16)

    kk = small_ref[:, 512:640]
    mu = jnp.mean(kk, axis=-1, keepdims=True)
    var = jnp.mean(jnp.square(kk - mu), axis=-1, keepdims=True)
    kn = (kk - mu) * lax.rsqrt(var + EPS) * lng_ref[...] + lnb_ref[...]
    kr = _rope(kn, c_i, sn_i, sp_i, 8)
    khi = kr.astype(_BF16).astype(_F32)
    kpair = jnp.where(first, khi, kr - khi).astype(_BF16)
    kcat_ref[:, 0:LANES] = kpair
    kcat_ref[:, LANES:2 * LANES] = kpair

    g = small_ref[:, 640:768]
    w_s = g * (IDX_HEADS ** -0.5)
    beta = jax.nn.sigmoid(g)
    z = g + dtb_ref[...]
    softplus = jnp.maximum(z, 0.0) + jnp.log1p(jnp.exp(-jnp.abs(z)))
    gg = -jnp.exp(alog_ref[...]) * softplus
    rowc = lax.broadcasted_iota(jnp.int32, (tr, LANES), 0) % CHUNK
    fwd = gg
    rev = gg
    s = 1
    while s < CHUNK:
        fwd = fwd + jnp.where(rowc >= s, pltpu.roll(fwd, s, 0), 0.0)
        rev = rev + jnp.where(rowc < CHUNK - s, pltpu.roll(rev, tr - s, 0), 0.0)
        s *= 2
    glast = fwd + rev - gg
    out = jnp.where(lane < 8, w_s, jnp.where(lane < 16, beta, jnp.where(lane < 24, fwd, 0.0)))
    out = jnp.where((lane >= 24) & (lane < 32), pltpu.roll(glast, 8, 1), out)
    scal_ref[...] = out


def _prep(big, small, pos2, fa, fi, lng, lnb, alog, dtb):
    m = big.shape[0]
    tr = 512
    row = lambda i: (i, 0)
    const = lambda i: (0, 0)
    return pl.pallas_call(
        _prep_kernel,
        grid=(m // tr,),
        in_specs=[pl.BlockSpec((tr, D_MODEL), lambda i: (i, CB_AQ)),
                  pl.BlockSpec((tr, D_MODEL), lambda i: (i, CB_AK)),
                  pl.BlockSpec((tr, SMALL_W), row),
                  pl.BlockSpec((tr, 1), row),
                  pl.BlockSpec((1, LANES), const), pl.BlockSpec((1, LANES), const),
                  pl.BlockSpec((1, LANES), const), pl.BlockSpec((1, LANES), const),
                  pl.BlockSpec((1, LANES), const), pl.BlockSpec((1, LANES), const)],
        out_specs=[pl.BlockSpec((tr, D_MODEL), row), pl.BlockSpec((tr, D_MODEL), row),
                   pl.BlockSpec((tr, IDX_HEADS * MXU_DIM), row), pl.BlockSpec((tr, MXU_DIM), row),
                   pl.BlockSpec((tr, LANES), row)],
        out_shape=[jax.ShapeDtypeStruct((m, D_MODEL), _BF16), jax.ShapeDtypeStruct((m, D_MODEL), _BF16),
                   jax.ShapeDtypeStruct((m, IDX_HEADS * MXU_DIM), _BF16),
                   jax.ShapeDtypeStruct((m, MXU_DIM), _BF16),
                   jax.ShapeDtypeStruct((m, LANES), _F32)],
        compiler_params=_cparams(("parallel",), 48),
        name="prep",
    )(big, big, small, pos2, fa, fi, lng, lnb, alog, dtb)


def _dsa_kernel(topk, qcat_ref, kcat_ref, q_ref, k_ref, vt_ref, wt_ref, o_ref,
                keys_ref, bias_ref, s_ref):
    qi = pl.program_id(1)
    nk = qi + 1
    kt_rows = lax.broadcasted_iota(jnp.int32, (TQ, TQ), 0)
    q_cols = lax.broadcasted_iota(jnp.int32, (TQ, TQ), 1)

    def score_tile(kt, carry):
        r0 = pl.multiple_of(kt * TQ, TQ)
        kc = kcat_ref[pl.ds(r0, TQ), :]
        acc = jnp.zeros((TQ, TQ), _F32)
        for h in range(IDX_HEADS):
            lg = _dot_nt(kc, qcat_ref[:, h * MXU_DIM:(h + 1) * MXU_DIM])
            acc = acc + wt_ref[h:h + 1, :] * jnp.maximum(lg, 0.0)
        bits = pltpu.bitcast(acc + 0.0, jnp.int32)
        key = bits ^ ((bits >> 31) & 0x7FFFFFFF)
        causal = (kt_rows + kt * TQ) <= (q_cols + qi * TQ)
        keys_ref[kt] = jnp.where(causal, key, INT_MIN)
        return carry

    lax.fori_loop(0, nk, score_tile, 0)

    def count(pred_fn):
        def body(kt, acc):
            m = jnp.where(pred_fn(keys_ref[kt], kt), 1, 0)
            return acc + jnp.sum(m.reshape(TQ // 8, 8, TQ), axis=0)
        part = lax.fori_loop(0, nk, body, jnp.zeros((8, TQ), jnp.int32))
        return jnp.sum(part, axis=0, keepdims=True)

    cnt0 = count(lambda kv, kt: kv >= 0)
    ok0 = cnt0 >= topk
    thr = jnp.where(ok0, 0, INT_MIN)
    cnt_thr = jnp.where(ok0, cnt0, nk * TQ)

    def bit_step(it, carry):
        thr, cnt_thr = carry
        cand = thr + lax.shift_left(jnp.int32(1), 30 - it)
        cnt = count(lambda kv, kt: kv >= cand)
        ok = cnt >= topk
        return jnp.where(ok, cand, thr), jnp.where(ok, cnt, cnt_thr)

    thr, cnt_thr = lax.fori_loop(0, 31, bit_step, (thr, cnt_thr))

    tie = jnp.max(jnp.where((cnt_thr > topk) & (thr > INT_MIN), 1, 0)) > 0

    @pl.when(jnp.logical_not(tie))
    def _():
        def body(kt, carry):
            kv = keys_ref[kt]
            bias_ref[kt] = jnp.where((kv >= thr) & (kv > INT_MIN), 0.0, NEG_BIG)
            return carry
        lax.fori_loop(0, nk, body, 0)

    @pl.when(tie)
    def _():
        need = topk - count(lambda kv, kt: kv > thr)

        def idx_step(it, pos):
            cand = pos + lax.shift_left(jnp.int32(1), 20 - it)
            cnt = count(lambda kv, kt: (kv == thr) & ((kt_rows + kt * TQ) < cand))
            return jnp.where(cnt < need, cand, pos)

        pos = lax.fori_loop(0, 21, idx_step, jnp.zeros((1, TQ), jnp.int32))

        def body(kt, carry):
            kv = keys_ref[kt]
            sel = (kv > thr) | ((kv == thr) & ((kt_rows + kt * TQ) <= pos))
            bias_ref[kt] = jnp.where(sel & (kv > INT_MIN), 0.0, NEG_BIG)
            return carry
        lax.fori_loop(0, nk, body, 0)

    for h in range(N_HEADS):
        sl = slice(h * HEAD_DIM, (h + 1) * HEAD_DIM)
        qh = q_ref[:, sl]

        def s_tile(kt, mx):
            r0 = pl.multiple_of(kt * TQ, TQ)
            s = _dot_nt(k_ref[pl.ds(r0, TQ), sl], qh) + bias_ref[kt]
            s_ref[kt] = s
            return jnp.maximum(mx, jnp.max(s.reshape(TQ // 8, 8, TQ), axis=0))

        mx = lax.fori_loop(0, nk, s_tile, jnp.full((8, TQ), NEG_BIG, _F32))
        mx = jnp.max(mx, axis=0, keepdims=True)

        def pv_tile(kt, carry):
            acc, l = carry
            p = jnp.exp(s_ref[kt] - mx)
            l = l + jnp.sum(p.reshape(TQ // 8, 8, TQ), axis=0)
            acc = acc + _dot(vt_ref[kt, sl, :], p.astype(_BF16))
            return acc, l

        acc, l = lax.fori_loop(0, nk, pv_tile,
                               (jnp.zeros((HEAD_DIM, TQ), _F32), jnp.zeros((8, TQ), _F32)))
        l = jnp.sum(l, axis=0, keepdims=True)
        o_ref[:, sl] = (acc / l).T.astype(_BF16)


def _dsa(qcat, kcat, q_r, k_r, vt, wt, topk):
    bsz, seq, _ = q_r.shape
    nq = seq // TQ
    return pl.pallas_call(
        functools.partial(_dsa_kernel, topk),
        grid=(bsz, nq),
        in_specs=[pl.BlockSpec((None, TQ, IDX_HEADS * MXU_DIM), lambda b, i: (b, i, 0)),
                  pl.BlockSpec((None, seq, MXU_DIM), lambda b, i: (b, 0, 0)),
                  pl.BlockSpec((None, TQ, D_MODEL), lambda b, i: (b, i, 0)),
                  pl.BlockSpec((None, seq, D_MODEL), lambda b, i: (b, 0, 0)),
                  pl.BlockSpec((None, nq, D_MODEL, TQ), lambda b, i: (b, 0, 0, 0)),
                  pl.BlockSpec((None, IDX_HEADS, TQ), lambda b, i: (b, 0, i))],
        out_specs=pl.BlockSpec((None, TQ, D_MODEL), lambda b, i: (b, i, 0)),
        out_shape=jax.ShapeDtypeStruct((bsz, seq, D_MODEL), _BF16),
        scratch_shapes=[pltpu.VMEM((nq, TQ, TQ), jnp.int32),
                        pltpu.VMEM((nq, TQ, TQ), _F32),
                        pltpu.VMEM((nq, TQ, TQ), _F32)],
        compiler_params=_cparams(("parallel", "arbitrary"), 48),
        name="dsa",
    )(qcat, kcat, q_r, k_r, vt, wt)


def _dnprep_kernel(x_ref, w_ref, o_ref, pad_ref):
    j = pl.program_id(1)
    seq, cw = x_ref.shape
    rt = 256
    pad_ref[0:8, :] = jnp.zeros((8, cw), _F32)
    for r in range(seq // rt):
        pad_ref[8 + r * rt:8 + (r + 1) * rt, :] = x_ref[r * rt:(r + 1) * rt, :].astype(_F32)
    qk_scale = jnp.where(j < 2, HEAD_DIM ** -0.5, 1.0)
    is_qk = j < 4
    for r in range(seq // rt):
        acc = jnp.zeros((rt, cw), _F32)
        for t in range(CONV_WIDTH):
            off = 8 + r * rt - (CONV_WIDTH - 1) + t
            acc = acc + w_ref[t:t + 1, :] * pad_ref[off:off + rt, :]
        y = _silu(acc)
        for h in range(cw // HEAD_DIM):
            sl = slice(h * HEAD_DIM, (h + 1) * HEAD_DIM)
            yh = y[:, sl]
            nrm = lax.rsqrt(jnp.sum(yh * yh, axis=-1, keepdims=True) + EPS) * qk_scale
            o_ref[r * rt:(r + 1) * rt, sl] = (yh * jnp.where(is_qk, nrm, 1.0)).astype(_BF16)


def _dnprep(big3, conv_w):
    bsz, seq, _ = big3.shape
    cw = 512
    nblk = 3 * D_MODEL // cw
    cb = CB_BQ * D_MODEL // cw
    return pl.pallas_call(
        _dnprep_kernel,
        grid=(bsz, nblk),
        in_specs=[pl.BlockSpec((None, seq, cw), lambda b, j: (b, 0, cb + j)),
                  pl.BlockSpec((CONV_WIDTH, cw), lambda b, j: (0, j))],
        out_specs=pl.BlockSpec((None, seq, cw), lambda b, j: (b, 0, j)),
        out_shape=jax.ShapeDtypeStruct((bsz, seq, 3 * D_MODEL), _BF16),
        scratch_shapes=[pltpu.VMEM((seq + 8, cw), _F32)],
        compiler_params=_cparams(("parallel", "arbitrary"), 48),
        name="dnprep",
    )(big3, conv_w)


def _delta_kernel(q_ref, k_ref, v_ref, z_ref, scal_ref, scalt_ref, gain_ref, o_ref,
                  state_ref, vnew_ref):
    g_idx = pl.program_id(1)

    @pl.when(g_idx == 0)
    def _():
        state_ref[...] = jnp.zeros_like(state_ref)
        vnew_ref[...] = jnp.zeros_like(vnew_ref)

    ri = lax.broadcasted_iota(jnp.int32, (GROUP, GROUP), 0)
    ci = lax.broadcasted_iota(jnp.int32, (GROUP, GROUP), 1)
    same = (ri // CHUNK) == (ci // CHUNK)
    incl = same & (ri >= ci)
    strict = same & (ri > ci)
    lane_chunk = lax.broadcasted_iota(jnp.int32, (1, GROUP), 1) // CHUNK

    for h in range(N_HEADS):
        sl = slice(h * HEAD_DIM, (h + 1) * HEAD_DIM)
        q = q_ref[:, sl].astype(_F32)
        k_bf = k_ref[:, sl]
        k = k_bf.astype(_F32)
        v = v_ref[:, sl].astype(_F32)
        beta_c = scal_ref[:, 8 + h:9 + h]
        gc_c = scal_ref[:, 16 + h:17 + h]
        gl_c = scal_ref[:, 24 + h:25 + h]
        gc_r = scalt_ref[16 + h:17 + h, :]

        decay = jnp.exp(jnp.where(incl, gc_c - gc_r, -jnp.inf))
        kb = k * beta_c
        kb_bf = kb.astype(_BF16)
        a = _dot_nt(kb_bf, k_bf)
        lmat = jnp.where(strict, a * decay, 0.0)
        n_acc = jnp.where((ri // 2) == (ci // 2), -lmat, 0.0)
        blk = 4
        while blk <= CHUNK:
            off = ((ri // blk) == (ci // blk)) & ((ri // (blk // 2)) != (ci // (blk // 2)))
            l_off = jnp.where(off, lmat, 0.0)
            n_bf = n_acc.astype(_BF16)
            p = l_off + _dot(l_off.astype(_BF16), n_bf)
            n_acc = n_acc - p - _dot(n_bf, p.astype(_BF16))
            blk *= 2
        rhs = jnp.concatenate([v * beta_c, kb * jnp.exp(gc_c)], axis=1)
        uw = rhs + _dot(n_acc.astype(_BF16), rhs.astype(_BF16))
        u = uw[:, :HEAD_DIM]
        w = uw[:, HEAD_DIM:]
        intra = (_dot_nt(q.astype(_BF16), k_bf) * decay).astype(_BF16)
        qg = (q * jnp.exp(gc_c)).astype(_BF16)
        w_bf = w.astype(_BF16)
        kdec_t = (k * jnp.exp(gl_c - gc_c)).T

        st = state_ref[h]
        for n in range(GROUP // CHUNK):
            rs = slice(n * CHUNK, (n + 1) * CHUNK)
            st_bf = st.astype(_BF16)
            v_new = u[rs] - _dot(w_bf[rs], st_bf)
            vnew_ref[h, rs, :] = v_new
            vn_bf = vnew_ref[h].astype(_BF16)
            o_n = _dot(qg[rs], st_bf) + _dot(intra[rs], vn_bf)
            kd = jnp.where(lane_chunk == n, kdec_t, 0.0).astype(_BF16)
            gl = gl_c[n * CHUNK:n * CHUNK + 1, :]
            st = st * jnp.exp(gl) + _dot(kd, vn_bf)
            ms = jnp.mean(o_n * o_n, axis=-1, keepdims=True)
            zn = z_ref[rs, sl].astype(_F32)
            o_ref[rs, sl] = (o_n * lax.rsqrt(ms + EPS) * gain_ref[...] * _silu(zn)).astype(_BF16)
        state_ref[h] = st


def _delta(qkv, big3, scal3, scalt, gain):
    bsz, seq, _ = qkv.shape
    ng = seq // GROUP
    return pl.pallas_call(
        _delta_kernel,
        grid=(bsz, ng),
        in_specs=[pl.BlockSpec((None, GROUP, D_MODEL), lambda b, g: (b, g, 0)),
                  pl.BlockSpec((None, GROUP, D_MODEL), lambda b, g: (b, g, 1)),
                  pl.BlockSpec((None, GROUP, D_MODEL), lambda b, g: (b, g, 2)),
                  pl.BlockSpec((None, GROUP, D_MODEL), lambda b, g: (b, g, CB_BZ)),
                  pl.BlockSpec((None, GROUP, LANES), lambda b, g: (b, g, 0)),
                  pl.BlockSpec((None, 32, GROUP), lambda b, g: (b, 0, g)),
                  pl.BlockSpec((1, HEAD_DIM), lambda b, g: (0, 0))],
        out_specs=pl.BlockSpec((None, GROUP, D_MODEL), lambda b, g: (b, g, 0)),
        out_shape=jax.ShapeDtypeStruct((bsz, seq, D_MODEL), _BF16),
        scratch_shapes=[pltpu.VMEM((N_HEADS, HEAD_DIM, HEAD_DIM), _F32),
                        pltpu.VMEM((N_HEADS, GROUP, HEAD_DIM), _F32)],
        compiler_params=_cparams(("parallel", "arbitrary"), 48),
        name="delta",
    )(qkv, qkv, qkv, big3, scal3, scalt, gain)


def _out_kernel(oa_ref, az_ref, ob_ref, ga_ref, gb_ref, x_ref, mod_ref, gain_ref,
                wa_ref, wb_ref, wo_ref, o_ref):
    za = (oa_ref[...].astype(_F32) * _silu(az_ref[...].astype(_F32))).astype(_BF16)
    ya = _dot(za, wa_ref[...])
    yb = _dot(ob_ref[...], wb_ref[...])
    y = jax.nn.sigmoid(ga_ref[...].astype(_F32)) * ya + jax.nn.sigmoid(gb_ref[...].astype(_F32)) * yb
    out = _dot(y.astype(_BF16), wo_ref[...])
    ms = jnp.mean(out * out, axis=-1, keepdims=True)
    gate = mod_ref[0, :, 2 * D_MODEL:3 * D_MODEL]
    o_ref[...] = x_ref[...] + gate * (out * lax.rsqrt(ms + EPS) * gain_ref[...])


def _out(oa2, big, ob2, x2, mod3, post_gain, wa, wb, wo, seq):
    m = x2.shape[0]
    tm = 512
    per_b = seq // tm
    row = lambda i: (i, 0)
    const = lambda i: (0, 0)
    return pl.pallas_call(
        _out_kernel,
        grid=(m // tm,),
        in_specs=[pl.BlockSpec((tm, D_MODEL), row),
                  pl.BlockSpec((tm, D_MODEL), lambda i: (i, CB_AZ)),
                  pl.BlockSpec((tm, D_MODEL), row),
                  pl.BlockSpec((tm, D_MODEL), lambda i: (i, CB_GA)),
                  pl.BlockSpec((tm, D_MODEL), lambda i: (i, CB_GB)),
                  pl.BlockSpec((tm, D_MODEL), row),
                  pl.BlockSpec((1, 1, 3 * D_MODEL), lambda i: (i // per_b, 0, 0)),
                  pl.BlockSpec((1, D_MODEL), const),
                  pl.BlockSpec((D_MODEL, D_MODEL), const),
                  pl.BlockSpec((D_MODEL, D_MODEL), const),
                  pl.BlockSpec((D_MODEL, D_MODEL), const)],
        out_specs=pl.BlockSpec((tm, D_MODEL), row),
        out_shape=jax.ShapeDtypeStruct((m, D_MODEL), _F32),
        compiler_params=_cparams(("parallel",), 48),
        name="out",
    )(oa2, big, ob2, big, big, x2, mod3, post_gain, wa, wb, wo)


def _lane_table(inv_freq, period, rot):
    half = rot // 2
    lane = np.arange(LANES)
    idx = (lane % period) % half
    on = (lane % period) < rot
    return jnp.where(jnp.asarray(on), inv_freq[idx], 0.0).reshape(1, LANES)


def _pad_lanes(v, start):
    return jnp.zeros((1, LANES), _F32).at[0, start:start + v.shape[0]].set(v)


def _layer(x, c, positions, w_ada, b_ada, pre_gain, post_gain, w_in, ln_gain, ln_bias,
           conv_w, a_log, dt_bias, dn_gain, w_a_out, w_b_out, w_o):
    bsz, seq, d = x.shape
    m = bsz * seq
    topk = min(TOPK_MAX, seq // 4)

    pts = np.cumsum([0, 1024, 1024, 1024, 1024, 512, 64, 8, 1024, 1024, 1024, 1024, 8, 8, 1024, 1024])
    col = lambda i: w_in[:, pts[i]:pts[i + 1]]
    (aq, ak, av, az, iq, ik, iw, bq, bk, bv, bz, bbeta, ba, ga, gb) = [col(i) for i in range(15)]
    w_big = jnp.concatenate([aq, ak, az, bq, bk, bv, bz, ga, gb], axis=1).astype(_BF16)
    w_vt = av.T.astype(_BF16)
    w_small = jnp.concatenate(
        [iq, ik, ik, iw, bbeta, ba, jnp.zeros((d, SMALL_W - 664), _F32)], axis=1)
    ws_hi = w_small.astype(_BF16)
    ws_lo = (w_small - ws_hi.astype(_F32)).astype(_BF16)

    mod = _ada(c, w_ada, b_ada)
    mod3 = mod.reshape(bsz, 1, 3 * d)
    x2 = x.reshape(m, d)
    big, small, vt = _proj(x2, mod3, pre_gain.reshape(1, d), w_big, ws_hi, ws_lo, w_vt, seq)

    rot_a = HEAD_DIM // ROPE_FRACTION
    rot_i = IDX_DIM // ROPE_FRACTION
    invf_a = ROPE_THETA ** (-(jnp.arange(rot_a // 2, dtype=_F32) * 2.0 / rot_a))
    invf_i = ROPE_THETA ** (-(jnp.arange(rot_i // 2, dtype=_F32) * 2.0 / rot_i))
    fa = _lane_table(invf_a, HEAD_DIM, rot_a)
    fi = _lane_table(invf_i, IDX_DIM, rot_i)
    lng = jnp.concatenate([ln_gain, ln_gain]).reshape(1, LANES)
    lnb = jnp.concatenate([ln_bias, ln_bias]).reshape(1, LANES)
    q_r, k_r, qcat, kcat, scal = _prep(big, small, positions.reshape(m, 1), fa, fi, lng, lnb,
                                       _pad_lanes(a_log, 16), _pad_lanes(dt_bias, 16))

    scal3 = scal.reshape(bsz, seq, LANES)
    scalt = jnp.transpose(scal3[:, :, :32], (0, 2, 1))
    o_a = _dsa(qcat.reshape(bsz, seq, -1), kcat.reshape(bsz, seq, -1),
               q_r.reshape(bsz, seq, d), k_r.reshape(bsz, seq, d),
               vt.reshape(bsz, seq // TQ, d, TQ), scalt[:, 0:IDX_HEADS, :], topk)

    big3 = big.reshape(bsz, seq, N_BIG)
    qkv = _dnprep(big3, conv_w)
    o_b = _delta(qkv, big3, scal3, scalt, dn_gain.reshape(1, HEAD_DIM))

    y = _out(o_a.reshape(m, d), big, o_b.reshape(m, d), x2, mod3, post_gain.reshape(1, d),
             w_a_out.astype(_BF16), w_b_out.astype(_BF16), w_o.astype(_BF16), seq)
    return y.reshape(bsz, seq, d)


def kernel(x, c, positions, w_ada, b_ada, pre_norm_gain, post_norm_gain, w_in, idx_k_ln_gain,
           idx_k_ln_bias, dn_conv_w, dn_a_log, dn_dt_bias, dn_norm_gain, w_a_out, w_b_out, w_o):
    for layer in range(w_ada.shape[0]):
        x = _layer(x, c, positions, w_ada[layer], b_ada[layer], pre_norm_gain[layer],
                   post_norm_gain[layer], w_in[layer], idx_k_ln_gain[layer], idx_k_ln_bias[layer],
                   dn_conv_w[layer], dn_a_log[layer], dn_dt_bias[layer], dn_norm_gain[layer],
                   w_a_out[layer], w_b_out[layer], w_o[layer])
    return x
```

```python
import functools

import numpy as np
import jax
import jax.numpy as jnp
from jax import lax
from jax.experimental import pallas as pl
from jax.experimental.pallas import tpu as pltpu

D_MODEL = 1024
N_HEADS = 8
HEAD_DIM = 128
IDX_HEADS = 8
IDX_DIM = 64
TOPK_MAX = 256
CONV_WIDTH = 4
CHUNK = 64
ROPE_THETA = 500000.0
ROPE_FRACTION = 4
EPS = 1e-6

LANES = 128
MXU_DIM = 256
TQ = MXU_DIM
GROUP = MXU_DIM
SMALL_W = 768
NEG_BIG = -1e30
HEAD_GROUP = 4
INT_MIN = -2147483648

CB_AQ, CB_AK, CB_AZ, CB_BQ, CB_BK, CB_BV, CB_BZ, CB_GA, CB_GB = range(9)
N_BIG = 9 * D_MODEL

_F32 = jnp.float32
_BF16 = jnp.bfloat16


def _cparams(sem, vmem_mb):
    return pltpu.CompilerParams(dimension_semantics=sem, vmem_limit_bytes=vmem_mb * 1024 * 1024)


def _split2(a):
    hi = a.astype(_BF16)
    lo = (a - hi.astype(_F32)).astype(_BF16)
    return hi, lo


def _dot(a, b):
    return jnp.dot(a, b, preferred_element_type=_F32)


def _dot_nt(a, b):
    return lax.dot_general(a, b, (((1,), (1,)), ((), ())), preferred_element_type=_F32)


def _silu(x):
    return x * jax.nn.sigmoid(x)


def _ada_kernel(c_ref, w_ref, b_ref, o_ref):
    c1, c2 = _split2(c_ref[...])
    c3 = (c_ref[...] - c1.astype(_F32) - c2.astype(_F32)).astype(_BF16)
    w = w_ref[...]
    w1, w2 = _split2(w)
    w3 = (w - w1.astype(_F32) - w2.astype(_F32)).astype(_BF16)
    acc = _dot(c1, w3) + _dot(c2, w2) + _dot(c3, w1)
    acc = acc + _dot(c1, w2) + _dot(c2, w1)
    acc = acc + _dot(c1, w1)
    o_ref[...] = acc + b_ref[...]


def _ada(c, w_ada, b_ada):
    bsz = c.shape[0]
    n = w_ada.shape[1]
    tn = 512
    return pl.pallas_call(
        _ada_kernel,
        grid=(n // tn,),
        in_specs=[pl.BlockSpec((bsz, D_MODEL), lambda j: (0, 0)),
                  pl.BlockSpec((D_MODEL, tn), lambda j: (0, j)),
                  pl.BlockSpec((1, tn), lambda j: (0, j))],
        out_specs=pl.BlockSpec((bsz, tn), lambda j: (0, j)),
        out_shape=jax.ShapeDtypeStruct((bsz, n), _F32),
        compiler_params=_cparams(("arbitrary",), 32),
        name="ada",
    )(c, w_ada, b_ada.reshape(1, n))


def _proj_kernel(x_ref, mod_ref, gain_ref, wbig_ref, wsh_ref, wsl_ref, wvt_ref,
                 big_ref, small_ref, vt_ref, h_ref):
    j = pl.program_id(1)

    @pl.when(j == 0)
    def _():
        x = x_ref[...]
        ms = jnp.mean(x * x, axis=-1, keepdims=True)
        y = x * lax.rsqrt(ms + EPS) * gain_ref[...]
        shift = mod_ref[0, :, 0:D_MODEL]
        scale = mod_ref[0, :, D_MODEL:2 * D_MODEL]
        h = y * (1.0 + scale) + shift
        hi, lo = _split2(h)
        h_ref[...] = hi
        small_ref[...] = (_dot(hi, wsl_ref[...]) + _dot(lo, wsh_ref[...])) + _dot(hi, wsh_ref[...])
        vt_ref[...] = _dot_nt(wvt_ref[...], hi).astype(_BF16)

    big_ref[...] = _dot(h_ref[...], wbig_ref[...]).astype(_BF16)


def _proj(x2, mod3, pre_gain, w_big, ws_hi, ws_lo, w_vt, seq):
    m = x2.shape[0]
    tm, tn = 512, 1024
    per_b = seq // tm
    return pl.pallas_call(
        _proj_kernel,
        grid=(m // tm, N_BIG // tn),
        in_specs=[pl.BlockSpec((tm, D_MODEL), lambda i, j: (i, 0)),
                  pl.BlockSpec((1, 1, 3 * D_MODEL), lambda i, j: (i // per_b, 0, 0)),
                  pl.BlockSpec((1, D_MODEL), lambda i, j: (0, 0)),
                  pl.BlockSpec((D_MODEL, tn), lambda i, j: (0, j)),
                  pl.BlockSpec((D_MODEL, SMALL_W), lambda i, j: (0, 0)),
                  pl.BlockSpec((D_MODEL, SMALL_W), lambda i, j: (0, 0)),
                  pl.BlockSpec((D_MODEL, D_MODEL), lambda i, j: (0, 0))],
        out_specs=[pl.BlockSpec((tm, tn), lambda i, j: (i, j)),
                   pl.BlockSpec((tm, SMALL_W), lambda i, j: (i, 0)),
                   pl.BlockSpec((D_MODEL, tm), lambda i, j: (0, i))],
        out_shape=[jax.ShapeDtypeStruct((m, N_BIG), _BF16),
                   jax.ShapeDtypeStruct((m, SMALL_W), _F32),
                   jax.ShapeDtypeStruct((D_MODEL, m), _BF16)],
        scratch_shapes=[pltpu.VMEM((tm, D_MODEL), _BF16)],
        compiler_params=_cparams(("parallel", "arbitrary"), 48),
        name="proj",
    )(x2, mod3, pre_gain, w_big, ws_hi, ws_lo, w_vt)


def _rope(x, c, sn, sp, half):
    return x * c + pltpu.roll(x, LANES - half, 1) * sn + pltpu.roll(x, half, 1) * sp


def _prep_kernel(aq_ref, ak_ref, small_ref, pos_ref, fa_ref, fi_ref, lng_ref, lnb_ref,
                 alog_ref, dtb_ref, q_ref, k_ref, qcat_ref, kcat_ref, scal_ref):
    tr = aq_ref.shape[0]
    posf = pos_ref[...].astype(_F32)
    lane = lax.broadcasted_iota(jnp.int32, (1, LANES), 1)

    ang = posf * fa_ref[...]
    ca, sa = jnp.cos(ang), jnp.sin(ang)
    c_a = jnp.where(lane < 32, ca, 1.0)
    sn_a = jnp.where(lane < 16, -sa, 0.0)
    sp_a = jnp.where((lane >= 16) & (lane < 32), sa, 0.0)
    for h in range(N_HEADS):
        sl = slice(h * HEAD_DIM, (h + 1) * HEAD_DIM)
        xq = aq_ref[:, sl].astype(_F32)
        q_ref[:, sl] = (_rope(xq, c_a, sn_a, sp_a, 16) * (HEAD_DIM ** -0.5)).astype(_BF16)
        xk = ak_ref[:, sl].astype(_F32)
        k_ref[:, sl] = _rope(xk, c_a, sn_a, sp_a, 16).astype(_BF16)

    ang_i = posf * fi_ref[...]
    ci, si = jnp.cos(ang_i), jnp.sin(ang_i)
    l64 = lane % IDX_DIM
    c_i = jnp.where(l64 < 16, ci, 1.0)
    sn_i = jnp.where(l64 < 8, -si, 0.0)
    sp_i = jnp.where((l64 >= 8) & (l64 < 16), si, 0.0)
    first = lane < IDX_DIM
    for p in range(IDX_HEADS // 2):
        x = small_ref[:, p * LANES:(p + 1) * LANES]
        y = _rope(x, c_i, sn_i, sp_i, 8) * (IDX_DIM ** -0.5)
        hi = y.astype(_BF16).astype(_F32)
        lo = y - hi
        hi_sw = pltpu.roll(hi, IDX_DIM, 1)
        lo_sw = pltpu.roll(lo, IDX_DIM, 1)
        base = 2 * p * MXU_DIM
        qcat_ref[:, base:base + LANES] = jnp.where(first, hi, hi_sw).astype(_BF16)
        qcat_ref[:, base + LANES:base + 2 * LANES] = jnp.where(first, lo, lo_sw).astype(_BF16)
        qcat_ref[:, base + 2 * LANES:base + 3 * LANES] = jnp.where(first, hi_sw, hi).astype(_BF16)
        qcat_ref[:, base + 3 * LANES:base + 4 * LANES] = jnp.where(first, lo_sw, lo).astype(_BF16)

    kk = small_ref[:, 512:640]
    mu = jnp.mean(kk, axis=-1, keepdims=True)
    var = jnp.mean(jnp.square(kk - mu), axis=-1, keepdims=True)
    kn = (kk - mu) * lax.rsqrt(var + EPS) * lng_ref[...] + lnb_ref[...]
    kr = _rope(kn, c_i, sn_i, sp_i, 8)
    khi = kr.astype(_BF16).astype(_F32)
    kpair = jnp.where(first, khi, kr - khi).astype(_BF16)
    kcat_ref[:, 0:LANES] = kpair
    kcat_ref[:, LANES:2 * LANES] = kpair

    g = small_ref[:, 640:768]
    w_s = g * (IDX_HEADS ** -0.5)
    beta = jax.nn.sigmoid(g)
    z = g + dtb_ref[...]
    softplus = jnp.maximum(z, 0.0) + jnp.log1p(jnp.exp(-jnp.abs(z)))
    gg = -jnp.exp(alog_ref[...]) * softplus
    rowc = lax.broadcasted_iota(jnp.int32, (tr, LANES), 0) % CHUNK
    fwd = gg
    rev = gg
    s = 1
    while s < CHUNK:
        fwd = fwd + jnp.where(rowc >= s, pltpu.roll(fwd, s, 0), 0.0)
        rev = rev + jnp.where(rowc < CHUNK - s, pltpu.roll(rev, tr - s, 0), 0.0)
        s *= 2
    glast = fwd + rev - gg
    out = jnp.where(lane < 8, w_s, jnp.where(lane < 16, beta, jnp.where(lane < 24, fwd, 0.0)))
    out = jnp.where((lane >= 24) & (lane < 32), pltpu.roll(glast, 8, 1), out)
    scal_ref[...] = out


def _prep(big, small, pos2, fa, fi, lng, lnb, alog, dtb):
    m = big.shape[0]
    tr = 512
    row = lambda i: (i, 0)
    const = lambda i: (0, 0)
    return pl.pallas_call(
        _prep_kernel,
        grid=(m // tr,),
        in_specs=[pl.BlockSpec((tr, D_MODEL), lambda i: (i, CB_AQ)),
                  pl.BlockSpec((tr, D_MODEL), lambda i: (i, CB_AK)),
                  pl.BlockSpec((tr, SMALL_W), row),
                  pl.BlockSpec((tr, 1), row),
                  pl.BlockSpec((1, LANES), const), pl.BlockSpec((1, LANES), const),
                  pl.BlockSpec((1, LANES), const), pl.BlockSpec((1, LANES), const),
                  pl.BlockSpec((1, LANES), const), pl.BlockSpec((1, LANES), const)],
        out_specs=[pl.BlockSpec((tr, D_MODEL), row), pl.BlockSpec((tr, D_MODEL), row),
                   pl.BlockSpec((tr, IDX_HEADS * MXU_DIM), row), pl.BlockSpec((tr, MXU_DIM), row),
                   pl.BlockSpec((tr, LANES), row)],
        out_shape=[jax.ShapeDtypeStruct((m, D_MODEL), _BF16), jax.ShapeDtypeStruct((m, D_MODEL), _BF16),
                   jax.ShapeDtypeStruct((m, IDX_HEADS * MXU_DIM), _BF16),
                   jax.ShapeDtypeStruct((m, MXU_DIM), _BF16),
                   jax.ShapeDtypeStruct((m, LANES), _F32)],
        compiler_params=_cparams(("parallel",), 48),
        name="prep",
    )(big, big, small, pos2, fa, fi, lng, lnb, alog, dtb)


def _dsa_kernel(topk, qcat_ref, kcat_ref, q_ref, k_ref, vt_ref, wt_ref, o_ref,
                keys_ref, hi16_ref, lo16_ref, bias_ref, s_ref, p_ref):
    qi = pl.program_id(1)
    nk = qi + 1
    kt_rows = lax.broadcasted_iota(jnp.int32, (TQ, TQ), 0)
    q_cols = lax.broadcasted_iota(jnp.int32, (TQ, TQ), 1)

    def score_tile(kt, carry):
        r0 = pl.multiple_of(kt * TQ, TQ)
        kc = kcat_ref[pl.ds(r0, TQ), :]
        acc = jnp.zeros((TQ, TQ), _F32)
        for h in range(IDX_HEADS):
            lg = _dot_nt(kc, qcat_ref[:, h * MXU_DIM:(h + 1) * MXU_DIM])
            acc = acc + wt_ref[h:h + 1, :] * jnp.maximum(lg, 0.0)
        bits = pltpu.bitcast(acc + 0.0, jnp.int32)
        key = bits ^ ((bits >> 31) & 0x7FFFFFFF)
        causal = (kt_rows + kt * TQ) <= (q_cols + qi * TQ)
        key = jnp.where(causal, key, INT_MIN)
        keys_ref[kt] = key
        hi16_ref[kt] = (key >> 16).astype(jnp.int16)
        lo16_ref[kt] = ((key & 0xFFFF) - 32768).astype(jnp.int16)
        return carry

    lax.fori_loop(0, nk, score_tile, 0)

    def count(pred_fn):
        def body(kt, acc):
            m = jnp.where(pred_fn(keys_ref[kt], kt), 1, 0)
            return acc + jnp.sum(m.reshape(TQ // 8, 8, TQ), axis=0)
        part = lax.fori_loop(0, nk, body, jnp.zeros((8, TQ), jnp.int32))
        return jnp.sum(part, axis=0, keepdims=True)

    def count16(ref, pred_fn):
        def body(kt, acc):
            m = jnp.where(pred_fn(ref[kt]), jnp.int16(1), jnp.int16(0))
            for r in range(TQ // 16):
                acc = acc + m[r * 16:(r + 1) * 16]
            return acc
        part = lax.fori_loop(0, nk, body, jnp.zeros((16, TQ), jnp.int16))
        return jnp.sum(part.astype(jnp.int32), axis=0, keepdims=True)

    def search16(ref, want):
        cnt0 = count16(ref, lambda kv: kv >= jnp.int16(0))
        ok0 = cnt0 >= want
        thr = jnp.where(ok0, 0, -32768)
        cnt_thr = jnp.where(ok0, cnt0, nk * TQ)

        def bit_step(it, carry):
            thr, cnt_thr = carry
            cand = thr + lax.shift_left(jnp.int32(1), 14 - it)
            cand16 = cand.astype(jnp.int16)
            cnt = count16(ref, lambda kv: kv >= cand16)
            ok = cnt >= want
            return jnp.where(ok, cand, thr), jnp.where(ok, cnt, cnt_thr)

        return lax.fori_loop(0, 15, bit_step, (thr, cnt_thr))

    thr_hi, _ = search16(hi16_ref, topk)
    thr_hi16 = thr_hi.astype(jnp.int16)
    n_above = count16(hi16_ref, lambda kv: kv > thr_hi16)

    def keep_matching(kt, carry):
        lo16_ref[kt] = jnp.where(hi16_ref[kt] == thr_hi16, lo16_ref[kt], jnp.int16(-32768))
        return carry

    lax.fori_loop(0, nk, keep_matching, 0)
    thr_lo, cnt_lo = search16(lo16_ref, topk - n_above)
    thr = thr_hi * 65536 + (thr_lo + 32768)
    cnt_thr = n_above + cnt_lo

    tie = jnp.max(jnp.where((cnt_thr > topk) & (thr > INT_MIN), 1, 0)) > 0

    @pl.when(jnp.logical_not(tie))
    def _():
        def body(kt, carry):
            kv = keys_ref[kt]
            bias_ref[kt] = jnp.where((kv >= thr) & (kv > INT_MIN), 0.0, NEG_BIG)
            return carry
        lax.fori_loop(0, nk, body, 0)

    @pl.when(tie)
    def _():
        need = topk - count(lambda kv, kt: kv > thr)

        def idx_step(it, pos):
            cand = pos + lax.shift_left(jnp.int32(1), 20 - it)
            cnt = count(lambda kv, kt: (kv == thr) & ((kt_rows + kt * TQ) < cand))
            return jnp.where(cnt < need, cand, pos)

        pos = lax.fori_loop(0, 21, idx_step, jnp.zeros((1, TQ), jnp.int32))

        def body(kt, carry):
            kv = keys_ref[kt]
            sel = (kv > thr) | ((kv == thr) & ((kt_rows + kt * TQ) <= pos))
            bias_ref[kt] = jnp.where(sel & (kv > INT_MIN), 0.0, NEG_BIG)
            return carry
        lax.fori_loop(0, nk, body, 0)

    def zero_tail(kt, carry):
        r0 = pl.multiple_of(kt * TQ, TQ)
        for i in range(HEAD_GROUP):
            p_ref[i, pl.ds(r0, TQ), :] = jnp.zeros((TQ, TQ), _BF16)
        return carry

    lax.fori_loop(nk, pl.num_programs(1), zero_tail, 0)
    for hg in range(N_HEADS // HEAD_GROUP):
        hs = [hg * HEAD_GROUP + i for i in range(HEAD_GROUP)]
        sls = [slice(h * HEAD_DIM, (h + 1) * HEAD_DIM) for h in hs]
        qhs = [q_ref[:, sl] for sl in sls]

        def s_tile(kt, mxs):
            r0 = pl.multiple_of(kt * TQ, TQ)
            bias = bias_ref[kt]
            out = []
            for i in range(HEAD_GROUP):
                s = _dot_nt(k_ref[pl.ds(r0, TQ), sls[i]], qhs[i]) + bias
                s_ref[i, kt] = s
                out.append(jnp.maximum(mxs[i], jnp.max(s.reshape(TQ // 8, 8, TQ), axis=0)))
            return tuple(out)

        mxs = lax.fori_loop(0, nk, s_tile,
                            tuple(jnp.full((8, TQ), NEG_BIG, _F32) for _ in range(HEAD_GROUP)))
        mxs = [jnp.max(m, axis=0, keepdims=True) for m in mxs]

        def p_tile(kt, ls):
            r0 = pl.multiple_of(kt * TQ, TQ)
            out = []
            for i in range(HEAD_GROUP):
                p = jnp.exp(s_ref[i, kt] - mxs[i])
                out.append(ls[i] + jnp.sum(p.reshape(TQ // 8, 8, TQ), axis=0))
                p_ref[i, pl.ds(r0, TQ), :] = p.astype(_BF16)
            return tuple(out)

        ls = lax.fori_loop(0, nk, p_tile,
                           tuple(jnp.zeros((8, TQ), _F32) for _ in range(HEAD_GROUP)))
        for i in range(HEAD_GROUP):
            l = jnp.sum(ls[i], axis=0, keepdims=True)
            acc = _dot(vt_ref[sls[i], :], p_ref[i])
            o_ref[:, sls[i]] = (acc / l).T.astype(_BF16)


def _dsa(qcat, kcat, q_r, k_r, vt, wt, topk):
    bsz, seq, _ = q_r.shape
    nq = seq // TQ
    return pl.pallas_call(
        functools.partial(_dsa_kernel, topk),
        grid=(bsz, nq),
        in_specs=[pl.BlockSpec((None, TQ, IDX_HEADS * MXU_DIM), lambda b, i: (b, i, 0)),
                  pl.BlockSpec((None, seq, MXU_DIM), lambda b, i: (b, 0, 0)),
                  pl.BlockSpec((None, TQ, D_MODEL), lambda b, i: (b, i, 0)),
                  pl.BlockSpec((None, seq, D_MODEL), lambda b, i: (b, 0, 0)),
                  pl.BlockSpec((D_MODEL, seq), lambda b, i: (0, b)),
                  pl.BlockSpec((None, IDX_HEADS, TQ), lambda b, i: (b, 0, i))],
        out_specs=pl.BlockSpec((None, TQ, D_MODEL), lambda b, i: (b, i, 0)),
        out_shape=jax.ShapeDtypeStruct((bsz, seq, D_MODEL), _BF16),
        scratch_shapes=[pltpu.VMEM((nq, TQ, TQ), jnp.int32),
                        pltpu.VMEM((nq, TQ, TQ), jnp.int16),
                        pltpu.VMEM((nq, TQ, TQ), jnp.int16),
                        pltpu.VMEM((nq, TQ, TQ), _F32),
                        pltpu.VMEM((HEAD_GROUP, nq, TQ, TQ), _F32),
                        pltpu.VMEM((HEAD_GROUP, seq, TQ), _BF16)],
        compiler_params=_cparams(("parallel", "arbitrary"), 48),
        name="dsa",
    )(qcat, kcat, q_r, k_r, vt, wt)


def _dnprep_kernel(x_ref, w_ref, o_ref, pad_ref):
    j = pl.program_id(1)
    seq, cw = x_ref.shape
    rt = 256
    pad_ref[0:8, :] = jnp.zeros((8, cw), _F32)
    for r in range(seq // rt):
        pad_ref[8 + r * rt:8 + (r + 1) * rt, :] = x_ref[r * rt:(r + 1) * rt, :].astype(_F32)
    qk_scale = jnp.where(j < 2, HEAD_DIM ** -0.5, 1.0)
    is_qk = j < 4
    for r in range(seq // rt):
        acc = jnp.zeros((rt, cw), _F32)
        for t in range(CONV_WIDTH):
            off = 8 + r * rt - (CONV_WIDTH - 1) + t
            acc = acc + w_ref[t:t + 1, :] * pad_ref[off:off + rt, :]
        y = _silu(acc)
        for h in range(cw // HEAD_DIM):
            sl = slice(h * HEAD_DIM, (h + 1) * HEAD_DIM)
            yh = y[:, sl]
            nrm = lax.rsqrt(jnp.sum(yh * yh, axis=-1, keepdims=True) + EPS) * qk_scale
            o_ref[r * rt:(r + 1) * rt, sl] = (yh * jnp.where(is_qk, nrm, 1.0)).astype(_BF16)


def _dnprep(big3, conv_w):
    bsz, seq, _ = big3.shape
    cw = 512
    nblk = 3 * D_MODEL // cw
    cb = CB_BQ * D_MODEL // cw
    return pl.pallas_call(
        _dnprep_kernel,
        grid=(bsz, nblk),
        in_specs=[pl.BlockSpec((None, seq, cw), lambda b, j: (b, 0, cb + j)),
                  pl.BlockSpec((CONV_WIDTH, cw), lambda b, j: (0, j))],
        out_specs=pl.BlockSpec((None, seq, cw), lambda b, j: (b, 0, j)),
        out_shape=jax.ShapeDtypeStruct((bsz, seq, 3 * D_MODEL), _BF16),
        scratch_shapes=[pltpu.VMEM((seq + 8, cw), _F32)],
        compiler_params=_cparams(("parallel", "arbitrary"), 48),
        name="dnprep",
    )(big3, conv_w)


N_LEVELS = 5
M_INCL, M_STRICT, M_PAIR, M_EYE, M_OFF0 = 0, 1, 2, 3, 4


def _delta_kernel(q_ref, k_ref, v_ref, z_ref, scal_ref, scalt_ref, gain_ref, o_ref,
                  state_ref, vnew_ref, mask_ref, lbf_ref, tbf_ref, pbf_ref, rhs_ref,
                  intra_ref, u_ref, wq_ref, kdt_ref):
    g_idx = pl.program_id(1)
    heads = range(N_HEADS)
    n_chunks = GROUP // CHUNK

    @pl.when(g_idx == 0)
    def _():
        state_ref[...] = jnp.zeros_like(state_ref)
        vnew_ref[...] = jnp.zeros_like(vnew_ref)
        ri = lax.broadcasted_iota(jnp.int32, (GROUP, GROUP), 0)
        ci = lax.broadcasted_iota(jnp.int32, (GROUP, GROUP), 1)
        same = (ri // CHUNK) == (ci // CHUNK)
        mask_ref[M_INCL] = jnp.where(same & (ri >= ci), 0.0, -jnp.inf)
        mask_ref[M_STRICT] = jnp.where(same & (ri > ci), 1.0, 0.0)
        mask_ref[M_PAIR] = jnp.where((ri // 2) == (ci // 2), 1.0, 0.0)
        mask_ref[M_EYE] = jnp.where(ri == ci, 1.0, 0.0)
        for lv in range(N_LEVELS):
            blk = 4 << lv
            off = ((ri // blk) == (ci // blk)) & ((ri // (blk // 2)) != (ci // (blk // 2)))
            mask_ref[M_OFF0 + lv] = jnp.where(off, 1.0, 0.0)

    for h in heads:
        sl = slice(h * HEAD_DIM, (h + 1) * HEAD_DIM)
        q = q_ref[:, sl].astype(_F32)
        k_bf = k_ref[:, sl]
        k = k_bf.astype(_F32)
        v = v_ref[:, sl].astype(_F32)
        beta_c = scal_ref[:, 8 + h:9 + h]
        gc_c = scal_ref[:, 16 + h:17 + h]
        gl_c = scal_ref[:, 24 + h:25 + h]
        gc_r = scalt_ref[16 + h:17 + h, :]
        decay = jnp.exp((gc_c - gc_r) + mask_ref[M_INCL])
        kb = k * beta_c
        lmat = _dot_nt(kb.astype(_BF16), k_bf) * decay * mask_ref[M_STRICT]
        lbf_ref[h] = lmat.astype(_BF16)
        tbf_ref[h] = (mask_ref[M_EYE] - lmat * mask_ref[M_PAIR]).astype(_BF16)
        intra_ref[h] = (_dot_nt(q.astype(_BF16), k_bf) * decay).astype(_BF16)
        egc = jnp.exp(gc_c)
        rhs_ref[h, :, 0:HEAD_DIM] = (v * beta_c).astype(_BF16)
        rhs_ref[h, :, HEAD_DIM:2 * HEAD_DIM] = (kb * egc).astype(_BF16)
        qg = (q * egc).astype(_BF16)
        for n in range(n_chunks):
            wq_ref[h, n, CHUNK:2 * CHUNK, :] = qg[n * CHUNK:(n + 1) * CHUNK]
        kdt_ref[h] = (k * jnp.exp(gl_c - gc_c)).T.astype(_BF16)

    for lv in range(N_LEVELS):
        for h in heads:
            p = _dot(lbf_ref[h], tbf_ref[h]) * mask_ref[M_OFF0 + lv]
            pbf_ref[h] = p.astype(_BF16)
        for h in heads:
            t = tbf_ref[h]
            tbf_ref[h] = t - _dot(t, pbf_ref[h]).astype(_BF16)

    for h in heads:
        uw = _dot(tbf_ref[h], rhs_ref[h])
        u_ref[h] = uw[:, :HEAD_DIM]
        w_bf = uw[:, HEAD_DIM:].astype(_BF16)
        for n in range(n_chunks):
            wq_ref[h, n, 0:CHUNK, :] = w_bf[n * CHUNK:(n + 1) * CHUNK]

    lane_chunk = lax.broadcasted_iota(jnp.int32, (1, GROUP), 1) // CHUNK
    for n in range(n_chunks):
        rs = slice(n * CHUNK, (n + 1) * CHUNK)
        for h in heads:
            ws = _dot(wq_ref[h, n], state_ref[h].astype(_BF16))
            vnew_ref[h, rs, :] = (u_ref[h, rs, :] - ws[:CHUNK]).astype(_BF16)
            u_ref[h, rs, :] = ws[CHUNK:]
        for h in heads:
            sl = slice(h * HEAD_DIM, (h + 1) * HEAD_DIM)
            vn = vnew_ref[h]
            o_n = u_ref[h, rs, :] + _dot(intra_ref[h, rs, :], vn)
            kd = jnp.where(lane_chunk == n, kdt_ref[h], jnp.zeros((), _BF16))
            gl = scal_ref[n * CHUNK:n * CHUNK + 1, 24 + h:25 + h]
            state_ref[h] = state_ref[h] * jnp.exp(gl) + _dot(kd, vn)
            ms = jnp.mean(o_n * o_n, axis=-1, keepdims=True)
            zn = z_ref[rs, sl].astype(_F32)
            o_ref[rs, sl] = (o_n * lax.rsqrt(ms + EPS) * gain_ref[...] * _silu(zn)).astype(_BF16)


def _delta(qkv, big3, scal3, scalt, gain):
    bsz, seq, _ = qkv.shape
    ng = seq // GROUP
    hm = (N_HEADS, GROUP, GROUP)
    return pl.pallas_call(
        _delta_kernel,
        grid=(bsz, ng),
        in_specs=[pl.BlockSpec((None, GROUP, D_MODEL), lambda b, g: (b, g, 0)),
                  pl.BlockSpec((None, GROUP, D_MODEL), lambda b, g: (b, g, 1)),
                  pl.BlockSpec((None, GROUP, D_MODEL), lambda b, g: (b, g, 2)),
                  pl.BlockSpec((None, GROUP, D_MODEL), lambda b, g: (b, g, CB_BZ)),
                  pl.BlockSpec((None, GROUP, LANES), lambda b, g: (b, g, 0)),
                  pl.BlockSpec((None, 32, GROUP), lambda b, g: (b, 0, g)),
                  pl.BlockSpec((1, HEAD_DIM), lambda b, g: (0, 0))],
        out_specs=pl.BlockSpec((None, GROUP, D_MODEL), lambda b, g: (b, g, 0)),
        out_shape=jax.ShapeDtypeStruct((bsz, seq, D_MODEL), _BF16),
        scratch_shapes=[pltpu.VMEM((N_HEADS, HEAD_DIM, HEAD_DIM), _F32),
                        pltpu.VMEM((N_HEADS, GROUP, HEAD_DIM), _BF16),
                        pltpu.VMEM((M_OFF0 + N_LEVELS, GROUP, GROUP), _F32),
                        pltpu.VMEM(hm, _BF16),
                        pltpu.VMEM(hm, _BF16),
                        pltpu.VMEM(hm, _BF16),
                        pltpu.VMEM(hm, _BF16),
                        pltpu.VMEM(hm, _BF16),
                        pltpu.VMEM((N_HEADS, GROUP, HEAD_DIM), _F32),
                        pltpu.VMEM((N_HEADS, GROUP // CHUNK, 2 * CHUNK, HEAD_DIM), _BF16),
                        pltpu.VMEM((N_HEADS, HEAD_DIM, GROUP), _BF16)],
        compiler_params=_cparams(("parallel", "arbitrary"), 48),
        name="delta",
    )(qkv, qkv, qkv, big3, scal3, scalt, gain)


def _out_kernel(oa_ref, az_ref, ob_ref, ga_ref, gb_ref, x_ref, mod_ref, gain_ref,
                wa_ref, wb_ref, wo_ref, o_ref):
    za = (oa_ref[...].astype(_F32) * _silu(az_ref[...].astype(_F32))).astype(_BF16)
    ya = _dot(za, wa_ref[...])
    yb = _dot(ob_ref[...], wb_ref[...])
    y = jax.nn.sigmoid(ga_ref[...].astype(_F32)) * ya + jax.nn.sigmoid(gb_ref[...].astype(_F32)) * yb
    out = _dot(y.astype(_BF16), wo_ref[...])
    ms = jnp.mean(out * out, axis=-1, keepdims=True)
    gate = mod_ref[0, :, 2 * D_MODEL:3 * D_MODEL]
    o_ref[...] = x_ref[...] + gate * (out * lax.rsqrt(ms + EPS) * gain_ref[...])


def _out(oa2, big, ob2, x2, mod3, post_gain, wa, wb, wo, seq):
    m = x2.shape[0]
    tm = 512
    per_b = seq // tm
    row = lambda i: (i, 0)
    const = lambda i: (0, 0)
    return pl.pallas_call(
        _out_kernel,
        grid=(m // tm,),
        in_specs=[pl.BlockSpec((tm, D_MODEL), row),
                  pl.BlockSpec((tm, D_MODEL), lambda i: (i, CB_AZ)),
                  pl.BlockSpec((tm, D_MODEL), row),
                  pl.BlockSpec((tm, D_MODEL), lambda i: (i, CB_GA)),
                  pl.BlockSpec((tm, D_MODEL), lambda i: (i, CB_GB)),
                  pl.BlockSpec((tm, D_MODEL), row),
                  pl.BlockSpec((1, 1, 3 * D_MODEL), lambda i: (i // per_b, 0, 0)),
                  pl.BlockSpec((1, D_MODEL), const),
                  pl.BlockSpec((D_MODEL, D_MODEL), const),
                  pl.BlockSpec((D_MODEL, D_MODEL), const),
                  pl.BlockSpec((D_MODEL, D_MODEL), const)],
        out_specs=pl.BlockSpec((tm, D_MODEL), row),
        out_shape=jax.ShapeDtypeStruct((m, D_MODEL), _F32),
        compiler_params=_cparams(("parallel",), 48),
        name="out",
    )(oa2, big, ob2, big, big, x2, mod3, post_gain, wa, wb, wo)


def _lane_table(inv_freq, period, rot):
    half = rot // 2
    lane = np.arange(LANES)
    idx = (lane % period) % half
    on = (lane % period) < rot
    return jnp.where(jnp.asarray(on), inv_freq[idx], 0.0).reshape(1, LANES)


def _pad_lanes(v, start):
    return jnp.zeros((1, LANES), _F32).at[0, start:start + v.shape[0]].set(v)


def _layer(x, c, positions, w_ada, b_ada, pre_gain, post_gain, w_in, ln_gain, ln_bias,
           conv_w, a_log, dt_bias, dn_gain, w_a_out, w_b_out, w_o):
    bsz, seq, d = x.shape
    m = bsz * seq
    topk = min(TOPK_MAX, seq // 4)

    pts = np.cumsum([0, 1024, 1024, 1024, 1024, 512, 64, 8, 1024, 1024, 1024, 1024, 8, 8, 1024, 1024])
    col = lambda i: w_in[:, pts[i]:pts[i + 1]]
    (aq, ak, av, az, iq, ik, iw, bq, bk, bv, bz, bbeta, ba, ga, gb) = [col(i) for i in range(15)]
    w_big = jnp.concatenate([aq, ak, az, bq, bk, bv, bz, ga, gb], axis=1).astype(_BF16)
    w_vt = av.T.astype(_BF16)
    w_small = jnp.concatenate(
        [iq, ik, ik, iw, bbeta, ba, jnp.zeros((d, SMALL_W - 664), _F32)], axis=1)
    ws_hi = w_small.astype(_BF16)
    ws_lo = (w_small - ws_hi.astype(_F32)).astype(_BF16)

    mod = _ada(c, w_ada, b_ada)
    mod3 = mod.reshape(bsz, 1, 3 * d)
    x2 = x.reshape(m, d)
    big, small, vt = _proj(x2, mod3, pre_gain.reshape(1, d), w_big, ws_hi, ws_lo, w_vt, seq)

    rot_a = HEAD_DIM // ROPE_FRACTION
    rot_i = IDX_DIM // ROPE_FRACTION
    invf_a = ROPE_THETA ** (-(jnp.arange(rot_a // 2, dtype=_F32) * 2.0 / rot_a))
    invf_i = ROPE_THETA ** (-(jnp.arange(rot_i // 2, dtype=_F32) * 2.0 / rot_i))
    fa = _lane_table(invf_a, HEAD_DIM, rot_a)
    fi = _lane_table(invf_i, IDX_DIM, rot_i)
    lng = jnp.concatenate([ln_gain, ln_gain]).reshape(1, LANES)
    lnb = jnp.concatenate([ln_bias, ln_bias]).reshape(1, LANES)
    q_r, k_r, qcat, kcat, scal = _prep(big, small, positions.reshape(m, 1), fa, fi, lng, lnb,
                                       _pad_lanes(a_log, 16), _pad_lanes(dt_bias, 16))

    scal3 = scal.reshape(bsz, seq, LANES)
    scalt = jnp.transpose(scal3[:, :, :32], (0, 2, 1))
    o_a = _dsa(qcat.reshape(bsz, seq, -1), kcat.reshape(bsz, seq, -1),
               q_r.reshape(bsz, seq, d), k_r.reshape(bsz, seq, d),
               vt, scalt[:, 0:IDX_HEADS, :], topk)

    big3 = big.reshape(bsz, seq, N_BIG)
    qkv = _dnprep(big3, conv_w)
    o_b = _delta(qkv, big3, scal3, scalt, dn_gain.reshape(1, HEAD_DIM))

    y = _out(o_a.reshape(m, d), big, o_b.reshape(m, d), x2, mod3, post_gain.reshape(1, d),
             w_a_out.astype(_BF16), w_b_out.astype(_BF16), w_o.astype(_BF16), seq)
    return y.reshape(bsz, seq, d)


def kernel(x, c, positions, w_ada, b_ada, pre_norm_gain, post_norm_gain, w_in, idx_k_ln_gain,
           idx_k_ln_bias, dn_conv_w, dn_a_log, dn_dt_bias, dn_norm_gain, w_a_out, w_b_out, w_o):
    for layer in range(w_ada.shape[0]):
        x = _layer(x, c, positions, w_ada[layer], b_ada[layer], pre_norm_gain[layer],
                   post_norm_gain[layer], w_in[layer], idx_k_ln_gain[layer], idx_k_ln_bias[layer],
                   dn_conv_w[layer], dn_a_log[layer], dn_dt_bias[layer], dn_norm_gain[layer],
                   w_a_out[layer], w_b_out[layer], w_o[layer])
    return x
```

```python
import functools

import numpy as np
import jax
import jax.numpy as jnp
from jax import lax
from jax.experimental import pallas as pl
from jax.experimental.pallas import tpu as pltpu

D_MODEL = 1024
N_HEADS = 8
HEAD_DIM = 128
IDX_HEADS = 8
IDX_DIM = 64
TOPK_MAX = 256
CONV_WIDTH = 4
CHUNK = 64
ROPE_THETA = 500000.0
ROPE_FRACTION = 4
EPS = 1e-6

LANES = 128
MXU_DIM = 256
TQ = MXU_DIM
GROUP = MXU_DIM
SUB = 128
SMALL_W = 256
IQ_W = IDX_HEADS * IDX_DIM
NEG_BIG = -1e30
HEAD_GROUP = 4
INT_MIN = -2147483648

CB_AK, CB_AZ, CB_BQ, CB_BK, CB_BV, CB_BZ, CB_GA, CB_GB = range(8)
N_BIG = 8 * D_MODEL

_F32 = jnp.float32
_BF16 = jnp.bfloat16


def _cparams(sem, vmem_mb):
    return pltpu.CompilerParams(dimension_semantics=sem, vmem_limit_bytes=vmem_mb * 1024 * 1024)


def _split2(a):
    hi = a.astype(_BF16)
    lo = (a - hi.astype(_F32)).astype(_BF16)
    return hi, lo


def _dot(a, b):
    return jnp.dot(a, b, preferred_element_type=_F32)


def _dot_nt(a, b):
    return lax.dot_general(a, b, (((1,), (1,)), ((), ())), preferred_element_type=_F32)


def _silu(x):
    return x * jax.nn.sigmoid(x)


def _ada_kernel(c_ref, w_ref, b_ref, o_ref):
    c1, c2 = _split2(c_ref[...])
    c3 = (c_ref[...] - c1.astype(_F32) - c2.astype(_F32)).astype(_BF16)
    w = w_ref[...]
    w1, w2 = _split2(w)
    w3 = (w - w1.astype(_F32) - w2.astype(_F32)).astype(_BF16)
    acc = _dot(c1, w3) + _dot(c2, w2) + _dot(c3, w1)
    acc = acc + _dot(c1, w2) + _dot(c2, w1)
    acc = acc + _dot(c1, w1)
    o_ref[...] = acc + b_ref[...]


def _ada(c, w_ada, b_ada):
    bsz = c.shape[0]
    n = w_ada.shape[1]
    tn = 512
    return pl.pallas_call(
        _ada_kernel,
        grid=(n // tn,),
        in_specs=[pl.BlockSpec((bsz, D_MODEL), lambda j: (0, 0)),
                  pl.BlockSpec((D_MODEL, tn), lambda j: (0, j)),
                  pl.BlockSpec((1, tn), lambda j: (0, j))],
        out_specs=pl.BlockSpec((bsz, tn), lambda j: (0, j)),
        out_shape=jax.ShapeDtypeStruct((bsz, n), _F32),
        compiler_params=_cparams(("arbitrary",), 32),
        name="ada",
    )(c, w_ada, b_ada.reshape(1, n))


def _proj_kernel(x_ref, mod_ref, gain_ref, wbig_ref, wsh_ref, wsl_ref, wqvt_ref, wiqh_ref, wiql_ref,
                 big_ref, small_ref, qt_ref, vt_ref, iqt_ref, h_ref):
    j = pl.program_id(1)

    @pl.when(j == 0)
    def _():
        x = x_ref[...]
        ms = jnp.mean(x * x, axis=-1, keepdims=True)
        y = x * lax.rsqrt(ms + EPS) * gain_ref[...]
        shift = mod_ref[0, :, 0:D_MODEL]
        scale = mod_ref[0, :, D_MODEL:2 * D_MODEL]
        h = y * (1.0 + scale) + shift
        hi, lo = _split2(h)
        h_ref[...] = hi
        small_ref[...] = (_dot(hi, wsl_ref[...]) + _dot(lo, wsh_ref[...])) + _dot(hi, wsh_ref[...])
        qv = _dot_nt(wqvt_ref[...], hi).astype(_BF16)
        qt_ref[...] = qv[:D_MODEL]
        vt_ref[...] = qv[D_MODEL:]
        iqt_ref[...] = (_dot_nt(wiql_ref[...], hi) + _dot_nt(wiqh_ref[...], lo)) + _dot_nt(wiqh_ref[...], hi)

    big_ref[...] = _dot(h_ref[...], wbig_ref[...]).astype(_BF16)


def _proj(x2, mod3, pre_gain, w_big, ws_hi, ws_lo, w_qvt, wiq_hi, wiq_lo, seq):
    m = x2.shape[0]
    tm, tn = 1024, 1024
    per_b = seq // tm
    const = lambda i, j: (0, 0)
    colblk = lambda i, j: (0, i)
    once = pl.Buffered(1)
    return pl.pallas_call(
        _proj_kernel,
        grid=(m // tm, N_BIG // tn),
        in_specs=[pl.BlockSpec((tm, D_MODEL), lambda i, j: (i, 0)),
                  pl.BlockSpec((1, 1, 3 * D_MODEL), lambda i, j: (i // per_b, 0, 0)),
                  pl.BlockSpec((1, D_MODEL), const),
                  pl.BlockSpec((D_MODEL, tn), lambda i, j: (0, j)),
                  pl.BlockSpec((D_MODEL, SMALL_W), const, pipeline_mode=once),
                  pl.BlockSpec((D_MODEL, SMALL_W), const, pipeline_mode=once),
                  pl.BlockSpec((2 * D_MODEL, D_MODEL), const, pipeline_mode=once),
                  pl.BlockSpec((IQ_W, D_MODEL), const, pipeline_mode=once),
                  pl.BlockSpec((IQ_W, D_MODEL), const, pipeline_mode=once)],
        out_specs=[pl.BlockSpec((tm, tn), lambda i, j: (i, j)),
                   pl.BlockSpec((tm, SMALL_W), lambda i, j: (i, 0)),
                   pl.BlockSpec((D_MODEL, tm), colblk),
                   pl.BlockSpec((D_MODEL, tm), colblk),
                   pl.BlockSpec((IQ_W, tm), colblk)],
        out_shape=[jax.ShapeDtypeStruct((m, N_BIG), _BF16),
                   jax.ShapeDtypeStruct((m, SMALL_W), _F32),
                   jax.ShapeDtypeStruct((D_MODEL, m), _BF16),
                   jax.ShapeDtypeStruct((D_MODEL, m), _BF16),
                   jax.ShapeDtypeStruct((IQ_W, m), _F32)],
        scratch_shapes=[pltpu.VMEM((tm, D_MODEL), _BF16)],
        compiler_params=_cparams(("parallel", "arbitrary"), 58),
        name="proj",
    )(x2, mod3, pre_gain, w_big, ws_hi, ws_lo, w_qvt, wiq_hi, wiq_lo)


def _rope(x, c, sn, sp, half):
    return x * c + pltpu.roll(x, LANES - half, 1) * sn + pltpu.roll(x, half, 1) * sp


def _prep_kernel(ak_ref, qt_ref, iqt_ref, small_ref, pos_ref, fa_ref, fi_ref, lng_ref, lnb_ref,
                 alog_ref, dtb_ref, qto_ref, k_ref, qcat_ref, kcat_ref, scal_ref):
    tr = ak_ref.shape[0]
    ha, hi_ = 16, 8
    posf = pos_ref[...].astype(_F32)
    ang_a = fa_ref[...] * posf
    cos_a, sin_a = jnp.cos(ang_a), jnp.sin(ang_a)
    ang_i = fi_ref[...] * posf
    cos_i, sin_i = jnp.cos(ang_i), jnp.sin(ang_i)

    for h in range(N_HEADS):
        b = h * HEAD_DIM
        x1 = qt_ref[b:b + ha, :].astype(_F32)
        x2 = qt_ref[b + ha:b + 2 * ha, :].astype(_F32)
        qto_ref[b:b + ha, :] = ((x1 * cos_a - x2 * sin_a) * (HEAD_DIM ** -0.5)).astype(_BF16)
        qto_ref[b + ha:b + 2 * ha, :] = ((x2 * cos_a + x1 * sin_a) * (HEAD_DIM ** -0.5)).astype(_BF16)
        rest = qt_ref[b + 2 * ha:b + HEAD_DIM, :].astype(_F32)
        qto_ref[b + 2 * ha:b + HEAD_DIM, :] = (rest * (HEAD_DIM ** -0.5)).astype(_BF16)

    for h in range(IDX_HEADS):
        b = h * IDX_DIM
        x1 = iqt_ref[b:b + hi_, :]
        x2 = iqt_ref[b + hi_:b + 2 * hi_, :]
        y = jnp.concatenate([x1 * cos_i - x2 * sin_i, x2 * cos_i + x1 * sin_i,
                             iqt_ref[b + 2 * hi_:b + IDX_DIM, :]], axis=0) * (IDX_DIM ** -0.5)
        yh = y.astype(_BF16)
        yl = (y - yh.astype(_F32)).astype(_BF16)
        o = h * MXU_DIM
        qcat_ref[o:o + IDX_DIM, :] = yh
        qcat_ref[o + IDX_DIM:o + 2 * IDX_DIM, :] = yh
        qcat_ref[o + 2 * IDX_DIM:o + 3 * IDX_DIM, :] = yl
        qcat_ref[o + 3 * IDX_DIM:o + 4 * IDX_DIM, :] = yl

    one_a = jnp.ones((LANES - 2 * ha, tr), _F32)
    zero = lambda n: jnp.zeros((n, tr), _F32)
    c_a = jnp.concatenate([cos_a, cos_a, one_a], axis=0).T
    sn_a = jnp.concatenate([-sin_a, zero(LANES - ha)], axis=0).T
    sp_a = jnp.concatenate([zero(ha), sin_a, zero(LANES - 2 * ha)], axis=0).T
    one_i = jnp.ones((IDX_DIM - 2 * hi_, tr), _F32)
    c_i = jnp.concatenate([cos_i, cos_i, one_i] * 2, axis=0).T
    sn_i = jnp.concatenate([-sin_i, zero(IDX_DIM - hi_)] * 2, axis=0).T
    sp_i = jnp.concatenate([zero(hi_), sin_i, zero(IDX_DIM - 2 * hi_)] * 2, axis=0).T

    for h in range(N_HEADS):
        sl = slice(h * HEAD_DIM, (h + 1) * HEAD_DIM)
        xk = ak_ref[:, sl].astype(_F32)
        k_ref[:, sl] = _rope(xk, c_a, sn_a, sp_a, ha).astype(_BF16)

    lane = lax.broadcasted_iota(jnp.int32, (1, LANES), 1)
    first = lane < IDX_DIM
    kk = small_ref[:, 0:LANES]
    mu = jnp.mean(kk, axis=-1, keepdims=True)
    var = jnp.mean(jnp.square(kk - mu), axis=-1, keepdims=True)
    kn = (kk - mu) * lax.rsqrt(var + EPS) * lng_ref[...] + lnb_ref[...]
    kr = _rope(kn, c_i, sn_i, sp_i, hi_)
    khi = kr.astype(_BF16).astype(_F32)
    kpair = jnp.where(first, khi, kr - khi).astype(_BF16)
    kcat_ref[:, 0:LANES] = kpair
    kcat_ref[:, LANES:2 * LANES] = kpair

    g = small_ref[:, LANES:2 * LANES]
    w_s = g * (IDX_HEADS ** -0.5)
    beta = jax.nn.sigmoid(g)
    z = g + dtb_ref[...]
    softplus = jnp.maximum(z, 0.0) + jnp.log1p(jnp.exp(-jnp.abs(z)))
    gg = -jnp.exp(alog_ref[...]) * softplus
    rowc = lax.broadcasted_iota(jnp.int32, (tr, LANES), 0) % CHUNK
    fwd = gg
    rev = gg
    s = 1
    while s < CHUNK:
        fwd = fwd + jnp.where(rowc >= s, pltpu.roll(fwd, s, 0), 0.0)
        rev = rev + jnp.where(rowc < CHUNK - s, pltpu.roll(rev, tr - s, 0), 0.0)
        s *= 2
    glast = fwd + rev - gg
    out = jnp.where(lane < 8, w_s, jnp.where(lane < 16, beta, jnp.where(lane < 24, fwd, 0.0)))
    out = jnp.where((lane >= 24) & (lane < 32), pltpu.roll(glast, 8, 1), out)
    scal_ref[...] = out


def _prep(big, qt, iqt, small, pos_row, fa, fi, lng, lnb, alog, dtb):
    m = big.shape[0]
    tr = 512
    row = lambda i: (i, 0)
    col = lambda i: (0, i)
    const = lambda i: (0, 0)
    return pl.pallas_call(
        _prep_kernel,
        grid=(m // tr,),
        in_specs=[pl.BlockSpec((tr, D_MODEL), lambda i: (i, CB_AK)),
                  pl.BlockSpec((D_MODEL, tr), col),
                  pl.BlockSpec((IQ_W, tr), col),
                  pl.BlockSpec((tr, SMALL_W), row),
                  pl.BlockSpec((1, tr), col),
                  pl.BlockSpec((16, 1), const), pl.BlockSpec((8, 1), const),
                  pl.BlockSpec((1, LANES), const), pl.BlockSpec((1, LANES), const),
                  pl.BlockSpec((1, LANES), const), pl.BlockSpec((1, LANES), const)],
        out_specs=[pl.BlockSpec((D_MODEL, tr), col), pl.BlockSpec((tr, D_MODEL), row),
                   pl.BlockSpec((IDX_HEADS * MXU_DIM, tr), col), pl.BlockSpec((tr, MXU_DIM), row),
                   pl.BlockSpec((tr, LANES), row)],
        out_shape=[jax.ShapeDtypeStruct((D_MODEL, m), _BF16), jax.ShapeDtypeStruct((m, D_MODEL), _BF16),
                   jax.ShapeDtypeStruct((IDX_HEADS * MXU_DIM, m), _BF16),
                   jax.ShapeDtypeStruct((m, MXU_DIM), _BF16),
                   jax.ShapeDtypeStruct((m, LANES), _F32)],
        compiler_params=_cparams(("parallel",), 48),
        name="prep",
    )(big, qt, iqt, small, pos_row, fa, fi, lng, lnb, alog, dtb)


def _dsa_kernel(topk, qcat_ref, kcat_ref, q_ref, k_ref, vt_ref, wt_ref, o_ref,
                keys_ref, hi16_ref, lo16_ref, bias_ref, s_ref, p_ref):
    qi = pl.program_id(1)
    nk = qi + 1
    kt_rows = lax.broadcasted_iota(jnp.int32, (TQ, TQ), 0)
    q_cols = lax.broadcasted_iota(jnp.int32, (TQ, TQ), 1)

    def score_tile(kt, carry):
        r0 = pl.multiple_of(kt * TQ, TQ)
        kc = kcat_ref[pl.ds(r0, TQ), :]
        for h in range(IDX_HEADS):
            lg = _dot(kc, qcat_ref[h * MXU_DIM:(h + 1) * MXU_DIM, :])
            term = wt_ref[h:h + 1, :] * jnp.maximum(lg, 0.0)
            if h == 0:
                bias_ref[kt] = term
            elif h < IDX_HEADS - 1:
                bias_ref[kt] += term
            else:
                acc = bias_ref[kt] + term
        bits = pltpu.bitcast(acc + 0.0, jnp.int32)
        key = bits ^ ((bits >> 31) & 0x7FFFFFFF)
        causal = (kt_rows + kt * TQ) <= (q_cols + qi * TQ)
        key = jnp.where(causal, key, INT_MIN)
        keys_ref[kt] = key
        hi16_ref[kt] = (key >> 16).astype(jnp.int16)
        lo16_ref[kt] = ((key & 0xFFFF) - 32768).astype(jnp.int16)
        return carry

    lax.fori_loop(0, nk, score_tile, 0)

    def count(pred_fn):
        def body(kt, acc):
            m = jnp.where(pred_fn(keys_ref[kt], kt), 1, 0)
            return acc + jnp.sum(m.reshape(TQ // 8, 8, TQ), axis=0)
        part = lax.fori_loop(0, nk, body, jnp.zeros((8, TQ), jnp.int32))
        return jnp.sum(part, axis=0, keepdims=True)

    def count16(ref, pred_fn):
        def body(kt, acc):
            m = jnp.where(pred_fn(ref[kt]), jnp.int16(1), jnp.int16(0))
            for r in range(TQ // 16):
                acc = acc + m[r * 16:(r + 1) * 16]
            return acc
        part = lax.fori_loop(0, nk, body, jnp.zeros((16, TQ), jnp.int16))
        return jnp.sum(part.astype(jnp.int32), axis=0, keepdims=True)

    def search16(ref, want):
        cnt0 = count16(ref, lambda kv: kv >= jnp.int16(0))
        ok0 = cnt0 >= want
        thr = jnp.where(ok0, 0, -32768)
        cnt_thr = jnp.where(ok0, cnt0, nk * TQ)

        def bit_step(it, carry):
            thr, cnt_thr = carry
            cand = thr + lax.shift_left(jnp.int32(1), 14 - it)
            cand16 = cand.astype(jnp.int16)
            cnt = count16(ref, lambda kv: kv >= cand16)
            ok = cnt >= want
            return jnp.where(ok, cand, thr), jnp.where(ok, cnt, cnt_thr)

        return lax.fori_loop(0, 15, bit_step, (thr, cnt_thr))

    thr_hi, _ = search16(hi16_ref, topk)
    thr_hi16 = thr_hi.astype(jnp.int16)
    n_above = count16(hi16_ref, lambda kv: kv > thr_hi16)

    def keep_matching(kt, carry):
        lo16_ref[kt] = jnp.where(hi16_ref[kt] == thr_hi16, lo16_ref[kt], jnp.int16(-32768))
        return carry

    lax.fori_loop(0, nk, keep_matching, 0)
    thr_lo, cnt_lo = search16(lo16_ref, topk - n_above)
    thr = thr_hi * 65536 + (thr_lo + 32768)
    cnt_thr = n_above + cnt_lo

    tie = jnp.max(jnp.where((cnt_thr > topk) & (thr > INT_MIN), 1, 0)) > 0

    @pl.when(jnp.logical_not(tie))
    def _():
        def body(kt, carry):
            kv = keys_ref[kt]
            bias_ref[kt] = jnp.where((kv >= thr) & (kv > INT_MIN), 0.0, NEG_BIG)
            return carry
        lax.fori_loop(0, nk, body, 0)

    @pl.when(tie)
    def _():
        need = topk - count(lambda kv, kt: kv > thr)

        def idx_step(it, pos):
            cand = pos + lax.shift_left(jnp.int32(1), 20 - it)
            cnt = count(lambda kv, kt: (kv == thr) & ((kt_rows + kt * TQ) < cand))
            return jnp.where(cnt < need, cand, pos)

        pos = lax.fori_loop(0, 21, idx_step, jnp.zeros((1, TQ), jnp.int32))

        def body(kt, carry):
            kv = keys_ref[kt]
            sel = (kv > thr) | ((kv == thr) & ((kt_rows + kt * TQ) <= pos))
            bias_ref[kt] = jnp.where(sel & (kv > INT_MIN), 0.0, NEG_BIG)
            return carry
        lax.fori_loop(0, nk, body, 0)

    def zero_tail(kt, carry):
        r0 = pl.multiple_of(kt * TQ, TQ)
        for i in range(HEAD_GROUP):
            p_ref[i, pl.ds(r0, TQ), :] = jnp.zeros((TQ, TQ), _BF16)
        return carry

    lax.fori_loop(nk, pl.num_programs(1), zero_tail, 0)
    for hg in range(N_HEADS // HEAD_GROUP):
        hs = [hg * HEAD_GROUP + i for i in range(HEAD_GROUP)]
        sls = [slice(h * HEAD_DIM, (h + 1) * HEAD_DIM) for h in hs]
        qhs = [q_ref[sl, :] for sl in sls]

        def s_tile(kt, mxs):
            r0 = pl.multiple_of(kt * TQ, TQ)
            bias = bias_ref[kt]
            out = []
            for i in range(HEAD_GROUP):
                s = _dot(k_ref[pl.ds(r0, TQ), sls[i]], qhs[i]) + bias
                s_ref[i, kt] = s
                out.append(jnp.maximum(mxs[i], jnp.max(s.reshape(TQ // 8, 8, TQ), axis=0)))
            return tuple(out)

        mxs = lax.fori_loop(0, nk, s_tile,
                            tuple(jnp.full((8, TQ), NEG_BIG, _F32) for _ in range(HEAD_GROUP)))
        mxs = [jnp.max(m, axis=0, keepdims=True) for m in mxs]

        def p_tile(kt, ls):
            r0 = pl.multiple_of(kt * TQ, TQ)
            out = []
            for i in range(HEAD_GROUP):
                p = jnp.exp(s_ref[i, kt] - mxs[i])
                out.append(ls[i] + jnp.sum(p.reshape(TQ // 8, 8, TQ), axis=0))
                p_ref[i, pl.ds(r0, TQ), :] = p.astype(_BF16)
            return tuple(out)

        ls = lax.fori_loop(0, nk, p_tile,
                           tuple(jnp.zeros((8, TQ), _F32) for _ in range(HEAD_GROUP)))
        for i in range(HEAD_GROUP):
            l = jnp.sum(ls[i], axis=0, keepdims=True)
            acc = _dot(vt_ref[sls[i], :], p_ref[i])
            o_ref[:, sls[i]] = (acc / l).T.astype(_BF16)


def _dsa(qcat_t, kcat, q_t, k_r, vt, wt, topk):
    bsz, seq, _ = k_r.shape
    nq = seq // TQ
    return pl.pallas_call(
        functools.partial(_dsa_kernel, topk),
        grid=(bsz, nq),
        in_specs=[pl.BlockSpec((IDX_HEADS * MXU_DIM, TQ), lambda b, i: (0, b * nq + i)),
                  pl.BlockSpec((None, seq, MXU_DIM), lambda b, i: (b, 0, 0)),
                  pl.BlockSpec((D_MODEL, TQ), lambda b, i: (0, b * nq + i)),
                  pl.BlockSpec((None, seq, D_MODEL), lambda b, i: (b, 0, 0)),
                  pl.BlockSpec((D_MODEL, seq), lambda b, i: (0, b)),
                  pl.BlockSpec((None, IDX_HEADS, TQ), lambda b, i: (b, 0, i))],
        out_specs=pl.BlockSpec((None, TQ, D_MODEL), lambda b, i: (b, i, 0)),
        out_shape=jax.ShapeDtypeStruct((bsz, seq, D_MODEL), _BF16),
        scratch_shapes=[pltpu.VMEM((nq, TQ, TQ), jnp.int32),
                        pltpu.VMEM((nq, TQ, TQ), jnp.int16),
                        pltpu.VMEM((nq, TQ, TQ), jnp.int16),
                        pltpu.VMEM((nq, TQ, TQ), _F32),
                        pltpu.VMEM((HEAD_GROUP, nq, TQ, TQ), _F32),
                        pltpu.VMEM((HEAD_GROUP, seq, TQ), _BF16)],
        compiler_params=_cparams(("parallel", "arbitrary"), 48),
        name="dsa",
    )(qcat_t, kcat, q_t, k_r, vt, wt)


def _dnprep_kernel(x_ref, w_ref, o_ref, pad_ref):
    j = pl.program_id(1)
    seq, cw = x_ref.shape
    rt = 256
    pad_ref[0:8, :] = jnp.zeros((8, cw), _F32)
    for r in range(seq // rt):
        pad_ref[8 + r * rt:8 + (r + 1) * rt, :] = x_ref[r * rt:(r + 1) * rt, :].astype(_F32)
    qk_scale = jnp.where(j < 2, HEAD_DIM ** -0.5, 1.0)
    is_qk = j < 4
    for r in range(seq // rt):
        win = pad_ref[r * rt:r * rt + rt + 8, :]
        acc = w_ref[CONV_WIDTH - 1:CONV_WIDTH, :] * win
        for t in range(CONV_WIDTH - 1):
            acc = acc + w_ref[t:t + 1, :] * pltpu.roll(win, CONV_WIDTH - 1 - t, 0)
        acc = acc[8:]
        y = _silu(acc)
        for h in range(cw // HEAD_DIM):
            sl = slice(h * HEAD_DIM, (h + 1) * HEAD_DIM)
            yh = y[:, sl]
            nrm = lax.rsqrt(jnp.sum(yh * yh, axis=-1, keepdims=True) + EPS) * qk_scale
            o_ref[r * rt:(r + 1) * rt, sl] = (yh * jnp.where(is_qk, nrm, 1.0)).astype(_BF16)


def _dnprep(big3, conv_w):
    bsz, seq, _ = big3.shape
    cw = 512
    nblk = 3 * D_MODEL // cw
    cb = CB_BQ * D_MODEL // cw
    return pl.pallas_call(
        _dnprep_kernel,
        grid=(bsz, nblk),
        in_specs=[pl.BlockSpec((None, seq, cw), lambda b, j: (b, 0, cb + j)),
                  pl.BlockSpec((CONV_WIDTH, cw), lambda b, j: (0, j))],
        out_specs=pl.BlockSpec((None, seq, cw), lambda b, j: (b, 0, j)),
        out_shape=jax.ShapeDtypeStruct((bsz, seq, 3 * D_MODEL), _BF16),
        scratch_shapes=[pltpu.VMEM((seq + 8, cw), _F32)],
        compiler_params=_cparams(("parallel", "arbitrary"), 48),
        name="dnprep",
    )(big3, conv_w)


N_LEVELS = 5
M_INCL, M_STRICT, M_PAIR, M_EYE, M_OFF0 = 0, 1, 2, 3, 4


def _delta_kernel(q_ref, k_ref, v_ref, z_ref, scal_ref, scalt_ref, gain_ref, o_ref,
                  state_ref, vnew_ref, mask_ref, lbf_ref, tbf_ref, pbf_ref, rhs_ref,
                  intra_ref, u_ref, wq_ref, kdt_ref):
    g_idx = pl.program_id(1)
    n_chunks = GROUP // CHUNK
    n_sub = GROUP // SUB
    per_sub = SUB // CHUNK
    units = [(h, b) for h in range(N_HEADS) for b in range(n_sub)]

    @pl.when(g_idx == 0)
    def _():
        state_ref[...] = jnp.zeros_like(state_ref)
        vnew_ref[...] = jnp.zeros_like(vnew_ref)
        ri = lax.broadcasted_iota(jnp.int32, (SUB, SUB), 0)
        ci = lax.broadcasted_iota(jnp.int32, (SUB, SUB), 1)
        same = (ri // CHUNK) == (ci // CHUNK)
        mask_ref[M_INCL] = jnp.where(same & (ri >= ci), 0.0, -jnp.inf)
        mask_ref[M_STRICT] = jnp.where(same & (ri > ci), 1.0, 0.0)
        mask_ref[M_PAIR] = jnp.where((ri // 2) == (ci // 2), 1.0, 0.0)
        mask_ref[M_EYE] = jnp.where(ri == ci, 1.0, 0.0)
        for lv in range(N_LEVELS):
            blk = 4 << lv
            off = ((ri // blk) == (ci // blk)) & ((ri // (blk // 2)) != (ci // (blk // 2)))
            mask_ref[M_OFF0 + lv] = jnp.where(off, 1.0, 0.0)

    for h, b in units:
        sl = slice(h * HEAD_DIM, (h + 1) * HEAD_DIM)
        rb = slice(b * SUB, (b + 1) * SUB)
        q = q_ref[rb, sl].astype(_F32)
        k_bf = k_ref[rb, sl]
        k = k_bf.astype(_F32)
        v = v_ref[rb, sl].astype(_F32)
        beta_c = scal_ref[rb, 8 + h:9 + h]
        gc_c = scal_ref[rb, 16 + h:17 + h]
        gl_c = scal_ref[rb, 24 + h:25 + h]
        gc_r = scalt_ref[16 + h:17 + h, rb]
        decay = jnp.exp((gc_c - gc_r) + mask_ref[M_INCL])
        kb = k * beta_c
        lmat = _dot_nt(kb.astype(_BF16), k_bf) * decay * mask_ref[M_STRICT]
        lbf_ref[h, b] = lmat.astype(_BF16)
        tbf_ref[h, b] = (mask_ref[M_EYE] - lmat * mask_ref[M_PAIR]).astype(_BF16)
        intra_ref[h, b] = (_dot_nt(q.astype(_BF16), k_bf) * decay).astype(_BF16)
        egc = jnp.exp(gc_c)
        rhs_ref[h, rb, 0:HEAD_DIM] = (v * beta_c).astype(_BF16)
        rhs_ref[h, rb, HEAD_DIM:2 * HEAD_DIM] = (kb * egc).astype(_BF16)
        qg = (q * egc).astype(_BF16)
        for c in range(per_sub):
            wq_ref[h, b * per_sub + c, CHUNK:2 * CHUNK, :] = qg[c * CHUNK:(c + 1) * CHUNK]
        kdt_ref[h, b] = (k * jnp.exp(gl_c - gc_c)).T.astype(_BF16)

    for lv in range(N_LEVELS):
        for h, b in units:
            p = _dot(lbf_ref[h, b], tbf_ref[h, b]) * mask_ref[M_OFF0 + lv]
            pbf_ref[h, b] = p.astype(_BF16)
        for h, b in units:
            t = tbf_ref[h, b]
            tbf_ref[h, b] = t - _dot(t, pbf_ref[h, b]).astype(_BF16)

    for h, b in units:
        rb = slice(b * SUB, (b + 1) * SUB)
        uw = _dot(tbf_ref[h, b], rhs_ref[h, rb, :])
        u_ref[h, rb, :] = uw[:, :HEAD_DIM]
        w_bf = uw[:, HEAD_DIM:].astype(_BF16)
        for c in range(per_sub):
            wq_ref[h, b * per_sub + c, 0:CHUNK, :] = w_bf[c * CHUNK:(c + 1) * CHUNK]

    lane_chunk = lax.broadcasted_iota(jnp.int32, (1, SUB), 1) // CHUNK
    for n in range(n_chunks):
        b, c = divmod(n, per_sub)
        rs = slice(n * CHUNK, (n + 1) * CHUNK)
        rc = slice(c * CHUNK, (c + 1) * CHUNK)
        for h in range(N_HEADS):
            ws = _dot(wq_ref[h, n], state_ref[h].astype(_BF16))
            vnew_ref[h, b, rc, :] = (u_ref[h, rs, :] - ws[:CHUNK]).astype(_BF16)
            u_ref[h, rs, :] = ws[CHUNK:]
        for h in range(N_HEADS):
            sl = slice(h * HEAD_DIM, (h + 1) * HEAD_DIM)
            vn = vnew_ref[h, b]
            o_n = u_ref[h, rs, :] + _dot(intra_ref[h, b, rc, :], vn)
            kd = jnp.where(lane_chunk == c, kdt_ref[h, b], jnp.zeros((), _BF16))
            gl = scal_ref[n * CHUNK:n * CHUNK + 1, 24 + h:25 + h]
            state_ref[h] = state_ref[h] * jnp.exp(gl) + _dot(kd, vn)
            ms = jnp.mean(o_n * o_n, axis=-1, keepdims=True)
            zn = z_ref[rs, sl].astype(_F32)
            o_ref[rs, sl] = (o_n * lax.rsqrt(ms + EPS) * gain_ref[...] * _silu(zn)).astype(_BF16)


def _delta(qkv, big3, scal3, scalt, gain):
    bsz, seq, _ = qkv.shape
    ng = seq // GROUP
    n_sub = GROUP // SUB
    hm = (N_HEADS, n_sub, SUB, SUB)
    return pl.pallas_call(
        _delta_kernel,
        grid=(bsz, ng),
        in_specs=[pl.BlockSpec((None, GROUP, D_MODEL), lambda b, g: (b, g, 0)),
                  pl.BlockSpec((None, GROUP, D_MODEL), lambda b, g: (b, g, 1)),
                  pl.BlockSpec((None, GROUP, D_MODEL), lambda b, g: (b, g, 2)),
                  pl.BlockSpec((None, GROUP, D_MODEL), lambda b, g: (b, g, CB_BZ)),
                  pl.BlockSpec((None, GROUP, LANES), lambda b, g: (b, g, 0)),
                  pl.BlockSpec((None, 32, GROUP), lambda b, g: (b, 0, g)),
                  pl.BlockSpec((1, HEAD_DIM), lambda b, g: (0, 0))],
        out_specs=pl.BlockSpec((None, GROUP, D_MODEL), lambda b, g: (b, g, 0)),
        out_shape=jax.ShapeDtypeStruct((bsz, seq, D_MODEL), _BF16),
        scratch_shapes=[pltpu.VMEM((N_HEADS, HEAD_DIM, HEAD_DIM), _F32),
                        pltpu.VMEM((N_HEADS, n_sub, SUB, HEAD_DIM), _BF16),
                        pltpu.VMEM((M_OFF0 + N_LEVELS, SUB, SUB), _F32),
                        pltpu.VMEM(hm, _BF16),
                        pltpu.VMEM(hm, _BF16),
                        pltpu.VMEM(hm, _BF16),
                        pltpu.VMEM((N_HEADS, GROUP, 2 * HEAD_DIM), _BF16),
                        pltpu.VMEM(hm, _BF16),
                        pltpu.VMEM((N_HEADS, GROUP, HEAD_DIM), _F32),
                        pltpu.VMEM((N_HEADS, GROUP // CHUNK, 2 * CHUNK, HEAD_DIM), _BF16),
                        pltpu.VMEM((N_HEADS, n_sub, HEAD_DIM, SUB), _BF16)],
        compiler_params=_cparams(("parallel", "arbitrary"), 48),
        name="delta",
    )(qkv, qkv, qkv, big3, scal3, scalt, gain)


def _out_kernel(oa_ref, az_ref, ob_ref, ga_ref, gb_ref, x_ref, mod_ref, gain_ref,
                wa_ref, wb_ref, wo_ref, o_ref):
    za = (oa_ref[...].astype(_F32) * _silu(az_ref[...].astype(_F32))).astype(_BF16)
    ya = _dot(za, wa_ref[...])
    yb = _dot(ob_ref[...], wb_ref[...])
    y = jax.nn.sigmoid(ga_ref[...].astype(_F32)) * ya + jax.nn.sigmoid(gb_ref[...].astype(_F32)) * yb
    out = _dot(y.astype(_BF16), wo_ref[...])
    ms = jnp.mean(out * out, axis=-1, keepdims=True)
    gate = mod_ref[0, :, 2 * D_MODEL:3 * D_MODEL]
    o_ref[...] = x_ref[...] + gate * (out * lax.rsqrt(ms + EPS) * gain_ref[...])


def _out(oa2, big, ob2, x2, mod3, post_gain, wa, wb, wo, seq):
    m = x2.shape[0]
    tm = 512
    per_b = seq // tm
    row = lambda i: (i, 0)
    const = lambda i: (0, 0)
    return pl.pallas_call(
        _out_kernel,
        grid=(m // tm,),
        in_specs=[pl.BlockSpec((tm, D_MODEL), row),
                  pl.BlockSpec((tm, D_MODEL), lambda i: (i, CB_AZ)),
                  pl.BlockSpec((tm, D_MODEL), row),
                  pl.BlockSpec((tm, D_MODEL), lambda i: (i, CB_GA)),
                  pl.BlockSpec((tm, D_MODEL), lambda i: (i, CB_GB)),
                  pl.BlockSpec((tm, D_MODEL), row),
                  pl.BlockSpec((1, 1, 3 * D_MODEL), lambda i: (i // per_b, 0, 0)),
                  pl.BlockSpec((1, D_MODEL), const),
                  pl.BlockSpec((D_MODEL, D_MODEL), const),
                  pl.BlockSpec((D_MODEL, D_MODEL), const),
                  pl.BlockSpec((D_MODEL, D_MODEL), const)],
        out_specs=pl.BlockSpec((tm, D_MODEL), row),
        out_shape=jax.ShapeDtypeStruct((m, D_MODEL), _F32),
        compiler_params=_cparams(("parallel",), 48),
        name="out",
    )(oa2, big, ob2, big, big, x2, mod3, post_gain, wa, wb, wo)


def _pad_lanes(v, start):
    return jnp.zeros((1, LANES), _F32).at[0, start:start + v.shape[0]].set(v)


def _layer(x, c, positions, w_ada, b_ada, pre_gain, post_gain, w_in, ln_gain, ln_bias,
           conv_w, a_log, dt_bias, dn_gain, w_a_out, w_b_out, w_o):
    bsz, seq, d = x.shape
    m = bsz * seq
    topk = min(TOPK_MAX, seq // 4)

    pts = np.cumsum([0, 1024, 1024, 1024, 1024, 512, 64, 8, 1024, 1024, 1024, 1024, 8, 8, 1024, 1024])
    col = lambda i: w_in[:, pts[i]:pts[i + 1]]
    (aq, ak, av, az, iq, ik, iw, bq, bk, bv, bz, bbeta, ba, ga, gb) = [col(i) for i in range(15)]
    w_big = jnp.concatenate([ak, az, bq, bk, bv, bz, ga, gb], axis=1).astype(_BF16)
    w_qvt = jnp.concatenate([aq, av], axis=1).T.astype(_BF16)
    w_small = jnp.concatenate(
        [ik, ik, iw, bbeta, ba, jnp.zeros((d, SMALL_W - 152), _F32)], axis=1)
    ws_hi = w_small.astype(_BF16)
    ws_lo = (w_small - ws_hi.astype(_F32)).astype(_BF16)
    iq_t = iq.T
    wiq_hi = iq_t.astype(_BF16)
    wiq_lo = (iq_t - wiq_hi.astype(_F32)).astype(_BF16)

    mod = _ada(c, w_ada, b_ada)
    mod3 = mod.reshape(bsz, 1, 3 * d)
    x2 = x.reshape(m, d)
    big, small, qt, vt, iqt = _proj(x2, mod3, pre_gain.reshape(1, d), w_big, ws_hi, ws_lo,
                                    w_qvt, wiq_hi, wiq_lo, seq)

    rot_a = HEAD_DIM // ROPE_FRACTION
    rot_i = IDX_DIM // ROPE_FRACTION
    invf_a = ROPE_THETA ** (-(jnp.arange(rot_a // 2, dtype=_F32) * 2.0 / rot_a))
    invf_i = ROPE_THETA ** (-(jnp.arange(rot_i // 2, dtype=_F32) * 2.0 / rot_i))
    lng = jnp.concatenate([ln_gain, ln_gain]).reshape(1, LANES)
    lnb = jnp.concatenate([ln_bias, ln_bias]).reshape(1, LANES)
    q_t, k_r, qcat_t, kcat, scal = _prep(big, qt, iqt, small, positions.reshape(1, m),
                                         invf_a.reshape(-1, 1), invf_i.reshape(-1, 1), lng, lnb,
                                         _pad_lanes(a_log, 16), _pad_lanes(dt_bias, 16))

    scal3 = scal.reshape(bsz, seq, LANES)
    scalt = jnp.transpose(scal3[:, :, :32], (0, 2, 1))
    o_a = _dsa(qcat_t, kcat.reshape(bsz, seq, -1), q_t, k_r.reshape(bsz, seq, d),
               vt, scalt[:, 0:IDX_HEADS, :], topk)

    big3 = big.reshape(bsz, seq, N_BIG)
    qkv = _dnprep(big3, conv_w)
    o_b = _delta(qkv, big3, scal3, scalt, dn_gain.reshape(1, HEAD_DIM))

    y = _out(o_a.reshape(m, d), big, o_b.reshape(m, d), x2, mod3, post_gain.reshape(1, d),
             w_a_out.astype(_BF16), w_b_out.astype(_BF16), w_o.astype(_BF16), seq)
    return y.reshape(bsz, seq, d)


def kernel(x, c, positions, w_ada, b_ada, pre_norm_gain, post_norm_gain, w_in, idx_k_ln_gain,
           idx_k_ln_bias, dn_conv_w, dn_a_log, dn_dt_bias, dn_norm_gain, w_a_out, w_b_out, w_o):
    for layer in range(w_ada.shape[0]):
        x = _layer(x, c, positions, w_ada[layer], b_ada[layer], pre_norm_gain[layer],
                   post_norm_gain[layer], w_in[layer], idx_k_ln_gain[layer], idx_k_ln_bias[layer],
                   dn_conv_w[layer], dn_a_log[layer], dn_dt_bias[layer], dn_norm_gain[layer],
                   w_a_out[layer], w_b_out[layer], w_o[layer])
    return x
```

```python
import functools

import numpy as np
import jax
import jax.numpy as jnp
from jax import lax
from jax.experimental import pallas as pl
from jax.experimental.pallas import tpu as pltpu

D_MODEL = 1024
N_HEADS = 8
HEAD_DIM = 128
IDX_HEADS = 8
IDX_DIM = 64
TOPK_MAX = 256
CONV_WIDTH = 4
CHUNK = 64
ROPE_THETA = 500000.0
ROPE_FRACTION = 4
EPS = 1e-6

LANES = 128
MXU_DIM = 256
TQ = MXU_DIM
GROUP = MXU_DIM
SUB = 128
SMALL_W = 256
IQ_W = IDX_HEADS * IDX_DIM
NEG_BIG = -1e30
HEAD_GROUP = 4
INT_MIN = -2147483648

CB_AK, CB_AZ, CB_BQ, CB_BK, CB_BV, CB_BZ, CB_GA, CB_GB = range(8)
N_BIG = 8 * D_MODEL

_F32 = jnp.float32
_BF16 = jnp.bfloat16


def _cparams(sem, vmem_mb):
    return pltpu.CompilerParams(dimension_semantics=sem, vmem_limit_bytes=vmem_mb * 1024 * 1024)


def _split2(a):
    hi = a.astype(_BF16)
    lo = (a - hi.astype(_F32)).astype(_BF16)
    return hi, lo


def _dot(a, b):
    return jnp.dot(a, b, preferred_element_type=_F32)


def _dot_nt(a, b):
    return lax.dot_general(a, b, (((1,), (1,)), ((), ())), preferred_element_type=_F32)


def _silu(x):
    return x * jax.nn.sigmoid(x)


def _ada_kernel(c_ref, w_ref, b_ref, o_ref):
    c1, c2 = _split2(c_ref[...])
    c3 = (c_ref[...] - c1.astype(_F32) - c2.astype(_F32)).astype(_BF16)
    w = w_ref[...]
    w1, w2 = _split2(w)
    w3 = (w - w1.astype(_F32) - w2.astype(_F32)).astype(_BF16)
    acc = _dot(c1, w3) + _dot(c2, w2) + _dot(c3, w1)
    acc = acc + _dot(c1, w2) + _dot(c2, w1)
    acc = acc + _dot(c1, w1)
    o_ref[...] = acc + b_ref[...]


def _ada(c, w_ada, b_ada):
    bsz = c.shape[0]
    n = w_ada.shape[1]
    tn = 512
    return pl.pallas_call(
        _ada_kernel,
        grid=(n // tn,),
        in_specs=[pl.BlockSpec((bsz, D_MODEL), lambda j: (0, 0)),
                  pl.BlockSpec((D_MODEL, tn), lambda j: (0, j)),
                  pl.BlockSpec((1, tn), lambda j: (0, j))],
        out_specs=pl.BlockSpec((bsz, tn), lambda j: (0, j)),
        out_shape=jax.ShapeDtypeStruct((bsz, n), _F32),
        compiler_params=_cparams(("arbitrary",), 32),
        name="ada",
    )(c, w_ada, b_ada.reshape(1, n))


def _proj_kernel(x_ref, mod_ref, gain_ref, wbig_ref, wsh_ref, wsl_ref, wqvt_ref, wiqh_ref, wiql_ref,
                 big_ref, small_ref, qt_ref, vt_ref, iqt_ref, h_ref):
    j = pl.program_id(1)

    @pl.when(j == 0)
    def _():
        x = x_ref[...]
        ms = jnp.mean(x * x, axis=-1, keepdims=True)
        y = x * lax.rsqrt(ms + EPS) * gain_ref[...]
        shift = mod_ref[0, :, 0:D_MODEL]
        scale = mod_ref[0, :, D_MODEL:2 * D_MODEL]
        h = y * (1.0 + scale) + shift
        hi, lo = _split2(h)
        h_ref[...] = hi
        small_ref[...] = (_dot(hi, wsl_ref[...]) + _dot(lo, wsh_ref[...])) + _dot(hi, wsh_ref[...])
        qv = _dot_nt(wqvt_ref[...], hi).astype(_BF16)
        qt_ref[...] = qv[:D_MODEL]
        vt_ref[...] = qv[D_MODEL:]
        iqt_ref[...] = (_dot_nt(wiql_ref[...], hi) + _dot_nt(wiqh_ref[...], lo)) + _dot_nt(wiqh_ref[...], hi)

    big_ref[...] = _dot(h_ref[...], wbig_ref[...]).astype(_BF16)


def _proj(x2, mod3, pre_gain, w_big, ws_hi, ws_lo, w_qvt, wiq_hi, wiq_lo, seq):
    m = x2.shape[0]
    tm, tn = min(1024, seq), 1024
    per_b = seq // tm
    const = lambda i, j: (0, 0)
    colblk = lambda i, j: (0, i)
    once = pl.Buffered(1)
    return pl.pallas_call(
        _proj_kernel,
        grid=(m // tm, N_BIG // tn),
        in_specs=[pl.BlockSpec((tm, D_MODEL), lambda i, j: (i, 0)),
                  pl.BlockSpec((1, 1, 3 * D_MODEL), lambda i, j: (i // per_b, 0, 0)),
                  pl.BlockSpec((1, D_MODEL), const),
                  pl.BlockSpec((D_MODEL, tn), lambda i, j: (0, j)),
                  pl.BlockSpec((D_MODEL, SMALL_W), const, pipeline_mode=once),
                  pl.BlockSpec((D_MODEL, SMALL_W), const, pipeline_mode=once),
                  pl.BlockSpec((2 * D_MODEL, D_MODEL), const, pipeline_mode=once),
                  pl.BlockSpec((IQ_W, D_MODEL), const, pipeline_mode=once),
                  pl.BlockSpec((IQ_W, D_MODEL), const, pipeline_mode=once)],
        out_specs=[pl.BlockSpec((tm, tn), lambda i, j: (i, j)),
                   pl.BlockSpec((tm, SMALL_W), lambda i, j: (i, 0)),
                   pl.BlockSpec((D_MODEL, tm), colblk),
                   pl.BlockSpec((D_MODEL, tm), colblk),
                   pl.BlockSpec((IQ_W, tm), colblk)],
        out_shape=[jax.ShapeDtypeStruct((m, N_BIG), _BF16),
                   jax.ShapeDtypeStruct((m, SMALL_W), _F32),
                   jax.ShapeDtypeStruct((D_MODEL, m), _BF16),
                   jax.ShapeDtypeStruct((D_MODEL, m), _BF16),
                   jax.ShapeDtypeStruct((IQ_W, m), _F32)],
        scratch_shapes=[pltpu.VMEM((tm, D_MODEL), _BF16)],
        compiler_params=_cparams(("parallel", "arbitrary"), 58),
        name="proj",
    )(x2, mod3, pre_gain, w_big, ws_hi, ws_lo, w_qvt, wiq_hi, wiq_lo)


def _rope(x, c, sn, sp, half):
    return x * c + pltpu.roll(x, LANES - half, 1) * sn + pltpu.roll(x, half, 1) * sp


def _prep_kernel(ak_ref, qt_ref, iqt_ref, small_ref, pos_ref, fa_ref, fi_ref, lng_ref, lnb_ref,
                 alog_ref, dtb_ref, qto_ref, k_ref, qcat_ref, kcat_ref, scal_ref):
    tr = ak_ref.shape[0]
    ha, hi_ = 16, 8
    posf = pos_ref[...].astype(_F32)
    ang_a = fa_ref[...] * posf
    cos_a, sin_a = jnp.cos(ang_a), jnp.sin(ang_a)
    ang_i = fi_ref[...] * posf
    cos_i, sin_i = jnp.cos(ang_i), jnp.sin(ang_i)

    for h in range(N_HEADS):
        b = h * HEAD_DIM
        x1 = qt_ref[b:b + ha, :].astype(_F32)
        x2 = qt_ref[b + ha:b + 2 * ha, :].astype(_F32)
        qto_ref[b:b + ha, :] = ((x1 * cos_a - x2 * sin_a) * (HEAD_DIM ** -0.5)).astype(_BF16)
        qto_ref[b + ha:b + 2 * ha, :] = ((x2 * cos_a + x1 * sin_a) * (HEAD_DIM ** -0.5)).astype(_BF16)
        rest = qt_ref[b + 2 * ha:b + HEAD_DIM, :].astype(_F32)
        qto_ref[b + 2 * ha:b + HEAD_DIM, :] = (rest * (HEAD_DIM ** -0.5)).astype(_BF16)

    for h in range(IDX_HEADS):
        b = h * IDX_DIM
        x1 = iqt_ref[b:b + hi_, :]
        x2 = iqt_ref[b + hi_:b + 2 * hi_, :]
        y = jnp.concatenate([x1 * cos_i - x2 * sin_i, x2 * cos_i + x1 * sin_i,
                             iqt_ref[b + 2 * hi_:b + IDX_DIM, :]], axis=0) * (IDX_DIM ** -0.5)
        yh = y.astype(_BF16)
        yl = (y - yh.astype(_F32)).astype(_BF16)
        o = h * MXU_DIM
        qcat_ref[o:o + IDX_DIM, :] = yh
        qcat_ref[o + IDX_DIM:o + 2 * IDX_DIM, :] = yh
        qcat_ref[o + 2 * IDX_DIM:o + 3 * IDX_DIM, :] = yl
        qcat_ref[o + 3 * IDX_DIM:o + 4 * IDX_DIM, :] = yl

    one_a = jnp.ones((LANES - 2 * ha, tr), _F32)
    zero = lambda n: jnp.zeros((n, tr), _F32)
    c_a = jnp.concatenate([cos_a, cos_a, one_a], axis=0).T
    sn_a = jnp.concatenate([-sin_a, zero(LANES - ha)], axis=0).T
    sp_a = jnp.concatenate([zero(ha), sin_a, zero(LANES - 2 * ha)], axis=0).T
    one_i = jnp.ones((IDX_DIM - 2 * hi_, tr), _F32)
    c_i = jnp.concatenate([cos_i, cos_i, one_i] * 2, axis=0).T
    sn_i = jnp.concatenate([-sin_i, zero(IDX_DIM - hi_)] * 2, axis=0).T
    sp_i = jnp.concatenate([zero(hi_), sin_i, zero(IDX_DIM - 2 * hi_)] * 2, axis=0).T

    for h in range(N_HEADS):
        sl = slice(h * HEAD_DIM, (h + 1) * HEAD_DIM)
        xk = ak_ref[:, sl].astype(_F32)
        k_ref[:, sl] = _rope(xk, c_a, sn_a, sp_a, ha).astype(_BF16)

    lane = lax.broadcasted_iota(jnp.int32, (1, LANES), 1)
    first = lane < IDX_DIM
    kk = small_ref[:, 0:LANES]
    mu = jnp.mean(kk, axis=-1, keepdims=True)
    var = jnp.mean(jnp.square(kk - mu), axis=-1, keepdims=True)
    kn = (kk - mu) * lax.rsqrt(var + EPS) * lng_ref[...] + lnb_ref[...]
    kr = _rope(kn, c_i, sn_i, sp_i, hi_)
    khi = kr.astype(_BF16).astype(_F32)
    kpair = jnp.where(first, khi, kr - khi).astype(_BF16)
    kcat_ref[:, 0:LANES] = kpair
    kcat_ref[:, LANES:2 * LANES] = kpair

    g = small_ref[:, LANES:2 * LANES]
    w_s = g * (IDX_HEADS ** -0.5)
    beta = jax.nn.sigmoid(g)
    z = g + dtb_ref[...]
    softplus = jnp.maximum(z, 0.0) + jnp.log1p(jnp.exp(-jnp.abs(z)))
    gg = -jnp.exp(alog_ref[...]) * softplus
    rowc = lax.broadcasted_iota(jnp.int32, (tr, LANES), 0) % CHUNK
    fwd = gg
    rev = gg
    s = 1
    while s < CHUNK:
        fwd = fwd + jnp.where(rowc >= s, pltpu.roll(fwd, s, 0), 0.0)
        rev = rev + jnp.where(rowc < CHUNK - s, pltpu.roll(rev, tr - s, 0), 0.0)
        s *= 2
    glast = fwd + rev - gg
    out = jnp.where(lane < 8, w_s, jnp.where(lane < 16, beta, jnp.where(lane < 24, fwd, 0.0)))
    out = jnp.where((lane >= 24) & (lane < 32), pltpu.roll(glast, 8, 1), out)
    scal_ref[...] = out


def _prep(big, qt, iqt, small, pos_row, fa, fi, lng, lnb, alog, dtb):
    m = big.shape[0]
    tr = 512
    row = lambda i: (i, 0)
    col = lambda i: (0, i)
    const = lambda i: (0, 0)
    return pl.pallas_call(
        _prep_kernel,
        grid=(m // tr,),
        in_specs=[pl.BlockSpec((tr, D_MODEL), lambda i: (i, CB_AK)),
                  pl.BlockSpec((D_MODEL, tr), col),
                  pl.BlockSpec((IQ_W, tr), col),
                  pl.BlockSpec((tr, SMALL_W), row),
                  pl.BlockSpec((1, tr), col),
                  pl.BlockSpec((16, 1), const), pl.BlockSpec((8, 1), const),
                  pl.BlockSpec((1, LANES), const), pl.BlockSpec((1, LANES), const),
                  pl.BlockSpec((1, LANES), const), pl.BlockSpec((1, LANES), const)],
        out_specs=[pl.BlockSpec((D_MODEL, tr), col), pl.BlockSpec((tr, D_MODEL), row),
                   pl.BlockSpec((IDX_HEADS * MXU_DIM, tr), col), pl.BlockSpec((tr, MXU_DIM), row),
                   pl.BlockSpec((tr, LANES), row)],
        out_shape=[jax.ShapeDtypeStruct((D_MODEL, m), _BF16), jax.ShapeDtypeStruct((m, D_MODEL), _BF16),
                   jax.ShapeDtypeStruct((IDX_HEADS * MXU_DIM, m), _BF16),
                   jax.ShapeDtypeStruct((m, MXU_DIM), _BF16),
                   jax.ShapeDtypeStruct((m, LANES), _F32)],
        compiler_params=_cparams(("parallel",), 48),
        name="prep",
    )(big, qt, iqt, small, pos_row, fa, fi, lng, lnb, alog, dtb)


def _tile_loop(nk, body, init):
    def pair(i, c):
        return body(2 * i + 1, body(2 * i, c))
    c = lax.fori_loop(0, nk // 2, pair, init)
    return lax.cond(nk % 2 == 1, lambda c: body(nk - 1, c), lambda c: c, c)


def _dsa_kernel(topk, qcat_ref, kcat_ref, q_ref, k_ref, vt_ref, wt_ref, o_ref,
                keys_ref, hi16_ref, lo16_ref, bias_ref, s_ref, p_ref, thr_ref):
    qi = pl.program_id(1)
    nk = qi + 1
    kt_rows = lax.broadcasted_iota(jnp.int32, (TQ, TQ), 0)
    row_minus_col = kt_rows - lax.broadcasted_iota(jnp.int32, (TQ, TQ), 1)

    def score_tile(kt, carry):
        r0 = pl.multiple_of(kt * TQ, TQ)
        kc = kcat_ref[pl.ds(r0, TQ), :]
        acc = jnp.zeros((TQ, TQ), _F32)
        for h in range(IDX_HEADS):
            lg = _dot(kc, qcat_ref[h * MXU_DIM:(h + 1) * MXU_DIM, :])
            acc = acc + wt_ref[h:h + 1, :] * jnp.maximum(lg, 0.0)
        bits = pltpu.bitcast(acc + 0.0, jnp.int32)
        key = bits ^ ((bits >> 31) & 0x7FFFFFFF)
        key = jnp.where(row_minus_col <= (qi - kt) * TQ, key, INT_MIN)
        keys_ref[kt] = key
        hi16_ref[kt] = (key >> 16).astype(jnp.int16)
        lo16_ref[kt] = (key - 32768).astype(jnp.int16)
        return carry

    _tile_loop(nk, score_tile, 0)

    def count(pred_fn):
        def body(kt, acc):
            m = jnp.where(pred_fn(keys_ref[kt], kt), 1, 0)
            return acc + jnp.sum(m.reshape(TQ // 8, 8, TQ), axis=0)
        part = lax.fori_loop(0, nk, body, jnp.zeros((8, TQ), jnp.int32))
        return jnp.sum(part, axis=0, keepdims=True)

    def search_static(n_tiles):
        def count16(ref, pred_fn):
            acc = jnp.zeros((16, TQ), jnp.int16)
            for kt in range(n_tiles):
                m = jnp.where(pred_fn(ref[kt]), jnp.int16(1), jnp.int16(0))
                for r in range(TQ // 16):
                    acc = acc + m[r * 16:(r + 1) * 16]
            return jnp.sum(acc.astype(jnp.int32), axis=0, keepdims=True)

        def search16(ref, want):
            cnt0 = count16(ref, lambda kv: kv >= jnp.int16(0))
            ok0 = cnt0 >= want
            thr = jnp.where(ok0, 0, -32768)
            cnt_thr = jnp.where(ok0, cnt0, n_tiles * TQ)

            def bit_step(it, carry):
                thr, cnt_thr = carry
                cand = thr + lax.shift_left(jnp.int32(1), 14 - it)
                cand16 = cand.astype(jnp.int16)
                cnt = count16(ref, lambda kv: kv >= cand16)
                ok = cnt >= want
                return jnp.where(ok, cand, thr), jnp.where(ok, cnt, cnt_thr)

            return lax.fori_loop(0, 15, bit_step, (thr, cnt_thr))

        thr_hi, _ = search16(hi16_ref, topk)
        thr_hi16 = thr_hi.astype(jnp.int16)
        n_above = count16(hi16_ref, lambda kv: kv > thr_hi16)
        for kt in range(n_tiles):
            lo16_ref[kt] = jnp.where(hi16_ref[kt] == thr_hi16, lo16_ref[kt], jnp.int16(-32768))
        thr_lo, cnt_lo = search16(lo16_ref, topk - n_above)
        thr_ref[0:1, :] = thr_hi * 65536 + (thr_lo + 32768)
        thr_ref[1:2, :] = n_above + cnt_lo

    for n_tiles in range(1, keys_ref.shape[0] + 1):
        pl.when(nk == n_tiles)(functools.partial(search_static, n_tiles))
    thr = thr_ref[0:1, :]
    cnt_thr = thr_ref[1:2, :]

    tie = jnp.max(jnp.where((cnt_thr > topk) & (thr > INT_MIN), 1, 0)) > 0

    @pl.when(jnp.logical_not(tie))
    def _():
        def body(kt, carry):
            kv = keys_ref[kt]
            bias_ref[kt] = jnp.where(kv >= thr_sel, 0.0, NEG_BIG)
            return carry
        thr_sel = jnp.maximum(thr, INT_MIN + 1)
        _tile_loop(nk, body, 0)

    @pl.when(tie)
    def _():
        need = topk - count(lambda kv, kt: kv > thr)

        def idx_step(it, pos):
            cand = pos + lax.shift_left(jnp.int32(1), 20 - it)
            cnt = count(lambda kv, kt: (kv == thr) & ((kt_rows + kt * TQ) < cand))
            return jnp.where(cnt < need, cand, pos)

        pos = lax.fori_loop(0, 21, idx_step, jnp.zeros((1, TQ), jnp.int32))

        def body(kt, carry):
            kv = keys_ref[kt]
            sel = (kv > thr) | ((kv == thr) & ((kt_rows + kt * TQ) <= pos))
            bias_ref[kt] = jnp.where(sel & (kv > INT_MIN), 0.0, NEG_BIG)
            return carry
        lax.fori_loop(0, nk, body, 0)

    def zero_tail(kt, carry):
        r0 = pl.multiple_of(kt * TQ, TQ)
        for i in range(HEAD_GROUP):
            p_ref[i, pl.ds(r0, TQ), :] = jnp.zeros((TQ, TQ), _BF16)
        return carry

    lax.fori_loop(nk, pl.num_programs(1), zero_tail, 0)
    for hg in range(N_HEADS // HEAD_GROUP):
        hs = [hg * HEAD_GROUP + i for i in range(HEAD_GROUP)]
        sls = [slice(h * HEAD_DIM, (h + 1) * HEAD_DIM) for h in hs]
        qhs = [q_ref[sl, :] for sl in sls]

        def s_tile(kt, mxs):
            r0 = pl.multiple_of(kt * TQ, TQ)
            bias = bias_ref[kt]
            out = []
            for i in range(HEAD_GROUP):
                s = _dot(k_ref[pl.ds(r0, TQ), sls[i]], qhs[i]) + bias
                s_ref[i, kt] = s
                out.append(jnp.maximum(mxs[i], jnp.max(s.reshape(TQ // 8, 8, TQ), axis=0)))
            return tuple(out)

        mxs = _tile_loop(nk, s_tile,
                            tuple(jnp.full((8, TQ), NEG_BIG, _F32) for _ in range(HEAD_GROUP)))
        mxs = [jnp.max(m, axis=0, keepdims=True) for m in mxs]

        def p_tile(kt, ls):
            r0 = pl.multiple_of(kt * TQ, TQ)
            out = []
            for i in range(HEAD_GROUP):
                p = jnp.exp(s_ref[i, kt] - mxs[i])
                out.append(ls[i] + jnp.sum(p.reshape(TQ // 8, 8, TQ), axis=0))
                p_ref[i, pl.ds(r0, TQ), :] = p.astype(_BF16)
            return tuple(out)

        ls = lax.fori_loop(0, nk, p_tile,
                           tuple(jnp.zeros((8, TQ), _F32) for _ in range(HEAD_GROUP)))
        for i in range(HEAD_GROUP):
            l = jnp.sum(ls[i], axis=0, keepdims=True)
            acc = _dot(vt_ref[sls[i], :], p_ref[i])
            o_ref[:, sls[i]] = (acc / l).T.astype(_BF16)


def _dsa(qcat_t, kcat, q_t, k_r, vt, wt, topk):
    bsz, seq, _ = k_r.shape
    nq = seq // TQ
    return pl.pallas_call(
        functools.partial(_dsa_kernel, topk),
        grid=(bsz, nq),
        in_specs=[pl.BlockSpec((IDX_HEADS * MXU_DIM, TQ), lambda b, i: (0, b * nq + i)),
                  pl.BlockSpec((None, seq, MXU_DIM), lambda b, i: (b, 0, 0)),
                  pl.BlockSpec((D_MODEL, TQ), lambda b, i: (0, b * nq + i)),
                  pl.BlockSpec((None, seq, D_MODEL), lambda b, i: (b, 0, 0)),
                  pl.BlockSpec((D_MODEL, seq), lambda b, i: (0, b)),
                  pl.BlockSpec((None, IDX_HEADS, TQ), lambda b, i: (b, 0, i))],
        out_specs=pl.BlockSpec((None, TQ, D_MODEL), lambda b, i: (b, i, 0)),
        out_shape=jax.ShapeDtypeStruct((bsz, seq, D_MODEL), _BF16),
        scratch_shapes=[pltpu.VMEM((nq, TQ, TQ), jnp.int32),
                        pltpu.VMEM((nq, TQ, TQ), jnp.int16),
                        pltpu.VMEM((nq, TQ, TQ), jnp.int16),
                        pltpu.VMEM((nq, TQ, TQ), _F32),
                        pltpu.VMEM((HEAD_GROUP, nq, TQ, TQ), _F32),
                        pltpu.VMEM((HEAD_GROUP, seq, TQ), _BF16),
                        pltpu.VMEM((8, TQ), jnp.int32)],
        compiler_params=_cparams(("parallel", "arbitrary"), 48),
        name="dsa",
    )(qcat_t, kcat, q_t, k_r, vt, wt)


def _dnprep_kernel(x_ref, w_ref, o_ref, pad_ref):
    j = pl.program_id(1)
    seq, cw = x_ref.shape
    rt = 256
    pad_ref[0:8, :] = jnp.zeros((8, cw), _F32)
    for r in range(seq // rt):
        pad_ref[8 + r * rt:8 + (r + 1) * rt, :] = x_ref[r * rt:(r + 1) * rt, :].astype(_F32)
    qk_scale = jnp.where(j < 2, HEAD_DIM ** -0.5, 1.0)
    is_qk = j < 4
    for r in range(seq // rt):
        win = pad_ref[r * rt:r * rt + rt + 8, :]
        acc = w_ref[CONV_WIDTH - 1:CONV_WIDTH, :] * win
        for t in range(CONV_WIDTH - 1):
            acc = acc + w_ref[t:t + 1, :] * pltpu.roll(win, CONV_WIDTH - 1 - t, 0)
        acc = acc[8:]
        y = _silu(acc)
        for h in range(cw // HEAD_DIM):
            sl = slice(h * HEAD_DIM, (h + 1) * HEAD_DIM)
            yh = y[:, sl]
            nrm = lax.rsqrt(jnp.sum(yh * yh, axis=-1, keepdims=True) + EPS) * qk_scale
            o_ref[r * rt:(r + 1) * rt, sl] = (yh * jnp.where(is_qk, nrm, 1.0)).astype(_BF16)


def _dnprep(big3, conv_w):
    bsz, seq, _ = big3.shape
    cw = 512
    nblk = 3 * D_MODEL // cw
    cb = CB_BQ * D_MODEL // cw
    return pl.pallas_call(
        _dnprep_kernel,
        grid=(bsz, nblk),
        in_specs=[pl.BlockSpec((None, seq, cw), lambda b, j: (b, 0, cb + j)),
                  pl.BlockSpec((CONV_WIDTH, cw), lambda b, j: (0, j))],
        out_specs=pl.BlockSpec((None, seq, cw), lambda b, j: (b, 0, j)),
        out_shape=jax.ShapeDtypeStruct((bsz, seq, 3 * D_MODEL), _BF16),
        scratch_shapes=[pltpu.VMEM((seq + 8, cw), _F32)],
        compiler_params=_cparams(("parallel", "arbitrary"), 48),
        name="dnprep",
    )(big3, conv_w)


N_LEVELS = 5
M_INCL, M_STRICT, M_PAIR, M_EYE, M_OFF0 = 0, 1, 2, 3, 4


def _delta_kernel(q_ref, k_ref, v_ref, z_ref, scal_ref, scalt_ref, gain_ref, o_ref,
                  state_ref, vnew_ref, mask_ref, lbf_ref, tbf_ref, pbf_ref, rhs_ref,
                  intra_ref, u_ref, wq_ref, kdt_ref):
    g_idx = pl.program_id(1)
    n_chunks = GROUP // CHUNK
    n_sub = GROUP // SUB
    per_sub = SUB // CHUNK
    units = [(h, b) for h in range(N_HEADS) for b in range(n_sub)]

    @pl.when(g_idx == 0)
    def _():
        state_ref[...] = jnp.zeros_like(state_ref)
        vnew_ref[...] = jnp.zeros_like(vnew_ref)
        ri = lax.broadcasted_iota(jnp.int32, (SUB, SUB), 0)
        ci = lax.broadcasted_iota(jnp.int32, (SUB, SUB), 1)
        same = (ri // CHUNK) == (ci // CHUNK)
        mask_ref[M_INCL] = jnp.where(same & (ri >= ci), 0.0, -jnp.inf)
        mask_ref[M_STRICT] = jnp.where(same & (ri > ci), 1.0, 0.0)
        mask_ref[M_PAIR] = jnp.where((ri // 2) == (ci // 2), 1.0, 0.0)
        mask_ref[M_EYE] = jnp.where(ri == ci, 1.0, 0.0)
        for lv in range(N_LEVELS):
            blk = 4 << lv
            off = ((ri // blk) == (ci // blk)) & ((ri // (blk // 2)) != (ci // (blk // 2)))
            mask_ref[M_OFF0 + lv] = jnp.where(off, 1.0, 0.0)

    for h, b in units:
        sl = slice(h * HEAD_DIM, (h + 1) * HEAD_DIM)
        rb = slice(b * SUB, (b + 1) * SUB)
        q = q_ref[rb, sl].astype(_F32)
        k_bf = k_ref[rb, sl]
        k = k_bf.astype(_F32)
        v = v_ref[rb, sl].astype(_F32)
        beta_c = scal_ref[rb, 8 + h:9 + h]
        gc_c = scal_ref[rb, 16 + h:17 + h]
        gl_c = scal_ref[rb, 24 + h:25 + h]
        gc_r = scalt_ref[16 + h:17 + h, rb]
        decay = jnp.exp((gc_c - gc_r) + mask_ref[M_INCL])
        kb = k * beta_c
        lmat = _dot_nt(kb.astype(_BF16), k_bf) * decay * mask_ref[M_STRICT]
        lbf_ref[h, b] = lmat.astype(_BF16)
        tbf_ref[h, b] = (mask_ref[M_EYE] - lmat * mask_ref[M_PAIR]).astype(_BF16)
        intra_ref[h, b] = (_dot_nt(q.astype(_BF16), k_bf) * decay).astype(_BF16)
        egc = jnp.exp(gc_c)
        rhs_ref[h, rb, 0:HEAD_DIM] = (v * beta_c).astype(_BF16)
        rhs_ref[h, rb, HEAD_DIM:2 * HEAD_DIM] = (kb * egc).astype(_BF16)
        qg = (q * egc).astype(_BF16)
        for c in range(per_sub):
            wq_ref[h, b * per_sub + c, CHUNK:2 * CHUNK, :] = qg[c * CHUNK:(c + 1) * CHUNK]
        kdt_ref[h, b] = (k * jnp.exp(gl_c - gc_c)).T.astype(_BF16)

    for lv in range(N_LEVELS):
        for h, b in units:
            p = _dot(lbf_ref[h, b], tbf_ref[h, b]) * mask_ref[M_OFF0 + lv]
            pbf_ref[h, b] = p.astype(_BF16)
        for h, b in units:
            t = tbf_ref[h, b]
            tbf_ref[h, b] = t - _dot(t, pbf_ref[h, b]).astype(_BF16)

    for h, b in units:
        rb = slice(b * SUB, (b + 1) * SUB)
        uw = _dot(tbf_ref[h, b], rhs_ref[h, rb, :])
        u_ref[h, rb, :] = uw[:, :HEAD_DIM]
        w_bf = uw[:, HEAD_DIM:].astype(_BF16)
        for c in range(per_sub):
            wq_ref[h, b * per_sub + c, 0:CHUNK, :] = w_bf[c * CHUNK:(c + 1) * CHUNK]

    lane_chunk = lax.broadcasted_iota(jnp.int32, (1, SUB), 1) // CHUNK
    for n in range(n_chunks):
        b, c = divmod(n, per_sub)
        rs = slice(n * CHUNK, (n + 1) * CHUNK)
        rc = slice(c * CHUNK, (c + 1) * CHUNK)
        for h in range(N_HEADS):
            ws = _dot(wq_ref[h, n], state_ref[h].astype(_BF16))
            vnew_ref[h, b, rc, :] = (u_ref[h, rs, :] - ws[:CHUNK]).astype(_BF16)
            u_ref[h, rs, :] = ws[CHUNK:]
        for h in range(N_HEADS):
            sl = slice(h * HEAD_DIM, (h + 1) * HEAD_DIM)
            vn = vnew_ref[h, b]
            o_n = u_ref[h, rs, :] + _dot(intra_ref[h, b, rc, :], vn)
            kd = jnp.where(lane_chunk == c, kdt_ref[h, b], jnp.zeros((), _BF16))
            gl = scal_ref[n * CHUNK:n * CHUNK + 1, 24 + h:25 + h]
            state_ref[h] = state_ref[h] * jnp.exp(gl) + _dot(kd, vn)
            ms = jnp.mean(o_n * o_n, axis=-1, keepdims=True)
            zn = z_ref[rs, sl].astype(_F32)
            o_ref[rs, sl] = (o_n * lax.rsqrt(ms + EPS) * gain_ref[...] * _silu(zn)).astype(_BF16)


def _delta(qkv, big3, scal3, scalt, gain):
    bsz, seq, _ = qkv.shape
    ng = seq // GROUP
    n_sub = GROUP // SUB
    hm = (N_HEADS, n_sub, SUB, SUB)
    return pl.pallas_call(
        _delta_kernel,
        grid=(bsz, ng),
        in_specs=[pl.BlockSpec((None, GROUP, D_MODEL), lambda b, g: (b, g, 0)),
                  pl.BlockSpec((None, GROUP, D_MODEL), lambda b, g: (b, g, 1)),
                  pl.BlockSpec((None, GROUP, D_MODEL), lambda b, g: (b, g, 2)),
                  pl.BlockSpec((None, GROUP, D_MODEL), lambda b, g: (b, g, CB_BZ)),
                  pl.BlockSpec((None, GROUP, LANES), lambda b, g: (b, g, 0)),
                  pl.BlockSpec((None, 32, GROUP), lambda b, g: (b, 0, g)),
                  pl.BlockSpec((1, HEAD_DIM), lambda b, g: (0, 0))],
        out_specs=pl.BlockSpec((None, GROUP, D_MODEL), lambda b, g: (b, g, 0)),
        out_shape=jax.ShapeDtypeStruct((bsz, seq, D_MODEL), _BF16),
        scratch_shapes=[pltpu.VMEM((N_HEADS, HEAD_DIM, HEAD_DIM), _F32),
                        pltpu.VMEM((N_HEADS, n_sub, SUB, HEAD_DIM), _BF16),
                        pltpu.VMEM((M_OFF0 + N_LEVELS, SUB, SUB), _F32),
                        pltpu.VMEM(hm, _BF16),
                        pltpu.VMEM(hm, _BF16),
                        pltpu.VMEM(hm, _BF16),
                        pltpu.VMEM((N_HEADS, GROUP, 2 * HEAD_DIM), _BF16),
                        pltpu.VMEM(hm, _BF16),
                        pltpu.VMEM((N_HEADS, GROUP, HEAD_DIM), _F32),
                        pltpu.VMEM((N_HEADS, GROUP // CHUNK, 2 * CHUNK, HEAD_DIM), _BF16),
                        pltpu.VMEM((N_HEADS, n_sub, HEAD_DIM, SUB), _BF16)],
        compiler_params=_cparams(("parallel", "arbitrary"), 48),
        name="delta",
    )(qkv, qkv, qkv, big3, scal3, scalt, gain)


def _out_kernel(oa_ref, az_ref, ob_ref, ga_ref, gb_ref, x_ref, mod_ref, gain_ref,
                wa_ref, wb_ref, wo_ref, o_ref):
    za = (oa_ref[...].astype(_F32) * _silu(az_ref[...].astype(_F32))).astype(_BF16)
    ya = _dot(za, wa_ref[...])
    yb = _dot(ob_ref[...], wb_ref[...])
    y = jax.nn.sigmoid(ga_ref[...].astype(_F32)) * ya + jax.nn.sigmoid(gb_ref[...].astype(_F32)) * yb
    out = _dot(y.astype(_BF16), wo_ref[...])
    ms = jnp.mean(out * out, axis=-1, keepdims=True)
    gate = mod_ref[0, :, 2 * D_MODEL:3 * D_MODEL]
    o_ref[...] = x_ref[...] + gate * (out * lax.rsqrt(ms + EPS) * gain_ref[...])


def _out(oa2, big, ob2, x2, mod3, post_gain, wa, wb, wo, seq):
    m = x2.shape[0]
    tm = 512
    per_b = seq // tm
    row = lambda i: (i, 0)
    const = lambda i: (0, 0)
    return pl.pallas_call(
        _out_kernel,
        grid=(m // tm,),
        in_specs=[pl.BlockSpec((tm, D_MODEL), row),
                  pl.BlockSpec((tm, D_MODEL), lambda i: (i, CB_AZ)),
                  pl.BlockSpec((tm, D_MODEL), row),
                  pl.BlockSpec((tm, D_MODEL), lambda i: (i, CB_GA)),
                  pl.BlockSpec((tm, D_MODEL), lambda i: (i, CB_GB)),
                  pl.BlockSpec((tm, D_MODEL), row),
                  pl.BlockSpec((1, 1, 3 * D_MODEL), lambda i: (i // per_b, 0, 0)),
                  pl.BlockSpec((1, D_MODEL), const),
                  pl.BlockSpec((D_MODEL, D_MODEL), const),
                  pl.BlockSpec((D_MODEL, D_MODEL), const),
                  pl.BlockSpec((D_MODEL, D_MODEL), const)],
        out_specs=pl.BlockSpec((tm, D_MODEL), row),
        out_shape=jax.ShapeDtypeStruct((m, D_MODEL), _F32),
        compiler_params=_cparams(("parallel",), 48),
        name="out",
    )(oa2, big, ob2, big, big, x2, mod3, post_gain, wa, wb, wo)


def _pad_lanes(v, start):
    return jnp.zeros((1, LANES), _F32).at[0, start:start + v.shape[0]].set(v)


def _layer(x, c, positions, w_ada, b_ada, pre_gain, post_gain, w_in, ln_gain, ln_bias,
           conv_w, a_log, dt_bias, dn_gain, w_a_out, w_b_out, w_o):
    bsz, seq, d = x.shape
    m = bsz * seq
    topk = min(TOPK_MAX, seq // 4)

    pts = np.cumsum([0, 1024, 1024, 1024, 1024, 512, 64, 8, 1024, 1024, 1024, 1024, 8, 8, 1024, 1024])
    col = lambda i: w_in[:, pts[i]:pts[i + 1]]
    (aq, ak, av, az, iq, ik, iw, bq, bk, bv, bz, bbeta, ba, ga, gb) = [col(i) for i in range(15)]
    w_big = jnp.concatenate([ak, az, bq, bk, bv, bz, ga, gb], axis=1).astype(_BF16)
    w_qvt = jnp.concatenate([aq, av], axis=1).T.astype(_BF16)
    w_small = jnp.concatenate(
        [ik, ik, iw, bbeta, ba, jnp.zeros((d, SMALL_W - 152), _F32)], axis=1)
    ws_hi = w_small.astype(_BF16)
    ws_lo = (w_small - ws_hi.astype(_F32)).astype(_BF16)
    iq_t = iq.T
    wiq_hi = iq_t.astype(_BF16)
    wiq_lo = (iq_t - wiq_hi.astype(_F32)).astype(_BF16)

    mod = _ada(c, w_ada, b_ada)
    mod3 = mod.reshape(bsz, 1, 3 * d)
    x2 = x.reshape(m, d)
    big, small, qt, vt, iqt = _proj(x2, mod3, pre_gain.reshape(1, d), w_big, ws_hi, ws_lo,
                                    w_qvt, wiq_hi, wiq_lo, seq)

    rot_a = HEAD_DIM // ROPE_FRACTION
    rot_i = IDX_DIM // ROPE_FRACTION
    invf_a = ROPE_THETA ** (-(jnp.arange(rot_a // 2, dtype=_F32) * 2.0 / rot_a))
    invf_i = ROPE_THETA ** (-(jnp.arange(rot_i // 2, dtype=_F32) * 2.0 / rot_i))
    lng = jnp.concatenate([ln_gain, ln_gain]).reshape(1, LANES)
    lnb = jnp.concatenate([ln_bias, ln_bias]).reshape(1, LANES)
    q_t, k_r, qcat_t, kcat, scal = _prep(big, qt, iqt, small, positions.reshape(1, m),
                                         invf_a.reshape(-1, 1), invf_i.reshape(-1, 1), lng, lnb,
                                         _pad_lanes(a_log, 16), _pad_lanes(dt_bias, 16))

    scal3 = scal.reshape(bsz, seq, LANES)
    scalt = jnp.transpose(scal3[:, :, :32], (0, 2, 1))
    o_a = _dsa(qcat_t, kcat.reshape(bsz, seq, -1), q_t, k_r.reshape(bsz, seq, d),
               vt, scalt[:, 0:IDX_HEADS, :], topk)

    big3 = big.reshape(bsz, seq, N_BIG)
    qkv = _dnprep(big3, conv_w)
    o_b = _delta(qkv, big3, scal3, scalt, dn_gain.reshape(1, HEAD_DIM))

    y = _out(o_a.reshape(m, d), big, o_b.reshape(m, d), x2, mod3, post_gain.reshape(1, d),
             w_a_out.astype(_BF16), w_b_out.astype(_BF16), w_o.astype(_BF16), seq)
    return y.reshape(bsz, seq, d)


def kernel(x, c, positions, w_ada, b_ada, pre_norm_gain, post_norm_gain, w_in, idx_k_ln_gain,
           idx_k_ln_bias, dn_conv_w, dn_a_log, dn_dt_bias, dn_norm_gain, w_a_out, w_b_out, w_o):
    for layer in range(w_ada.shape[0]):
        x = _layer(x, c, positions, w_ada[layer], b_ada[layer], pre_norm_gain[layer],
                   post_norm_gain[layer], w_in[layer], idx_k_ln_gain[layer], idx_k_ln_bias[layer],
                   dn_conv_w[layer], dn_a_log[layer], dn_dt_bias[layer], dn_norm_gain[layer],
                   w_a_out[layer], w_b_out[layer], w_o[layer])
    return x
```

```python
import functools

import numpy as np
import jax
import jax.numpy as jnp
from jax import lax
from jax.experimental import pallas as pl
from jax.experimental.pallas import tpu as pltpu

D_MODEL = 1024
N_HEADS = 8
HEAD_DIM = 128
IDX_HEADS = 8
IDX_DIM = 64
TOPK_MAX = 256
CONV_WIDTH = 4
CHUNK = 64
ROPE_THETA = 500000.0
ROPE_FRACTION = 4
EPS = 1e-6

LANES = 128
MXU_DIM = 256
TQ = MXU_DIM
GROUP = MXU_DIM
SUB = 128
SMALL_W = 256
IQ_W = IDX_HEADS * IDX_DIM
NEG_BIG = -1e30
HEAD_GROUP = 4
LOG2E = 1.4426950408889634
COUNT_LANES = 8
INT_MIN = -2147483648

CB_AK, CB_AZ, CB_BQ, CB_BK, CB_BV, CB_BZ, CB_GA, CB_GB = range(8)
N_BIG = 8 * D_MODEL

_F32 = jnp.float32
_BF16 = jnp.bfloat16


def _cparams(sem, vmem_mb):
    return pltpu.CompilerParams(dimension_semantics=sem, vmem_limit_bytes=vmem_mb * 1024 * 1024)


def _split2(a):
    hi = a.astype(_BF16)
    lo = (a - hi.astype(_F32)).astype(_BF16)
    return hi, lo


def _dot(a, b):
    return jnp.dot(a, b, preferred_element_type=_F32)


def _dot_nt(a, b):
    return lax.dot_general(a, b, (((1,), (1,)), ((), ())), preferred_element_type=_F32)


def _silu(x):
    return x * jax.nn.sigmoid(x)


def _ada_kernel(c_ref, w_ref, b_ref, o_ref):
    c1, c2 = _split2(c_ref[...])
    c3 = (c_ref[...] - c1.astype(_F32) - c2.astype(_F32)).astype(_BF16)
    w = w_ref[...]
    w1, w2 = _split2(w)
    w3 = (w - w1.astype(_F32) - w2.astype(_F32)).astype(_BF16)
    acc = _dot(c1, w3) + _dot(c2, w2) + _dot(c3, w1)
    acc = acc + _dot(c1, w2) + _dot(c2, w1)
    acc = acc + _dot(c1, w1)
    o_ref[...] = acc + b_ref[...]


def _ada(c, w_ada, b_ada):
    bsz = c.shape[0]
    n = w_ada.shape[1]
    tn = 512
    return pl.pallas_call(
        _ada_kernel,
        grid=(n // tn,),
        in_specs=[pl.BlockSpec((bsz, D_MODEL), lambda j: (0, 0)),
                  pl.BlockSpec((D_MODEL, tn), lambda j: (0, j)),
                  pl.BlockSpec((1, tn), lambda j: (0, j))],
        out_specs=pl.BlockSpec((bsz, tn), lambda j: (0, j)),
        out_shape=jax.ShapeDtypeStruct((bsz, n), _F32),
        compiler_params=_cparams(("arbitrary",), 32),
        name="ada",
    )(c, w_ada, b_ada.reshape(1, n))


def _proj_kernel(x_ref, mod_ref, gain_ref, wbig_ref, wsh_ref, wsl_ref, wqvt_ref, wiqh_ref, wiql_ref,
                 big_ref, small_ref, qt_ref, vt_ref, iqt_ref, h_ref):
    j = pl.program_id(1)

    @pl.when(j == 0)
    def _():
        x = x_ref[...]
        ms = jnp.mean(x * x, axis=-1, keepdims=True)
        y = x * lax.rsqrt(ms + EPS) * gain_ref[...]
        shift = mod_ref[0, :, 0:D_MODEL]
        scale = mod_ref[0, :, D_MODEL:2 * D_MODEL]
        h = y * (1.0 + scale) + shift
        hi, lo = _split2(h)
        h_ref[...] = hi
        small_ref[...] = (_dot(hi, wsl_ref[...]) + _dot(lo, wsh_ref[...])) + _dot(hi, wsh_ref[...])
        qv = _dot_nt(wqvt_ref[...], hi).astype(_BF16)
        qt_ref[...] = qv[:D_MODEL]
        vt_ref[...] = qv[D_MODEL:]
        iqt_ref[...] = (_dot_nt(wiql_ref[...], hi) + _dot_nt(wiqh_ref[...], lo)) + _dot_nt(wiqh_ref[...], hi)

    big_ref[...] = _dot(h_ref[...], wbig_ref[...]).astype(_BF16)


def _proj(x2, mod3, pre_gain, w_big, ws_hi, ws_lo, w_qvt, wiq_hi, wiq_lo, seq):
    m = x2.shape[0]
    tm, tn = min(1024, seq), 1024
    per_b = seq // tm
    const = lambda i, j: (0, 0)
    colblk = lambda i, j: (0, i)
    once = pl.Buffered(1)
    return pl.pallas_call(
        _proj_kernel,
        grid=(m // tm, N_BIG // tn),
        in_specs=[pl.BlockSpec((tm, D_MODEL), lambda i, j: (i, 0)),
                  pl.BlockSpec((1, 1, 3 * D_MODEL), lambda i, j: (i // per_b, 0, 0)),
                  pl.BlockSpec((1, D_MODEL), const),
                  pl.BlockSpec((D_MODEL, tn), lambda i, j: (0, j)),
                  pl.BlockSpec((D_MODEL, SMALL_W), const, pipeline_mode=once),
                  pl.BlockSpec((D_MODEL, SMALL_W), const, pipeline_mode=once),
                  pl.BlockSpec((2 * D_MODEL, D_MODEL), const, pipeline_mode=once),
                  pl.BlockSpec((IQ_W, D_MODEL), const, pipeline_mode=once),
                  pl.BlockSpec((IQ_W, D_MODEL), const, pipeline_mode=once)],
        out_specs=[pl.BlockSpec((tm, tn), lambda i, j: (i, j)),
                   pl.BlockSpec((tm, SMALL_W), lambda i, j: (i, 0)),
                   pl.BlockSpec((D_MODEL, tm), colblk),
                   pl.BlockSpec((D_MODEL, tm), colblk),
                   pl.BlockSpec((IQ_W, tm), colblk)],
        out_shape=[jax.ShapeDtypeStruct((m, N_BIG), _BF16),
                   jax.ShapeDtypeStruct((m, SMALL_W), _F32),
                   jax.ShapeDtypeStruct((D_MODEL, m), _BF16),
                   jax.ShapeDtypeStruct((D_MODEL, m), _BF16),
                   jax.ShapeDtypeStruct((IQ_W, m), _F32)],
        scratch_shapes=[pltpu.VMEM((tm, D_MODEL), _BF16)],
        compiler_params=_cparams(("parallel", "arbitrary"), 58),
        name="proj",
    )(x2, mod3, pre_gain, w_big, ws_hi, ws_lo, w_qvt, wiq_hi, wiq_lo)


def _rope(x, c, sn, sp, half):
    return x * c + pltpu.roll(x, LANES - half, 1) * sn + pltpu.roll(x, half, 1) * sp


def _prep_kernel(ak_ref, qt_ref, iqt_ref, small_ref, pos_ref, fa_ref, fi_ref, lng_ref, lnb_ref,
                 alog_ref, dtb_ref, qto_ref, k_ref, qcat_ref, kcat_ref, scal_ref):
    tr = ak_ref.shape[0]
    ha, hi_ = 16, 8
    posf = pos_ref[...].astype(_F32)
    ang_a = fa_ref[...] * posf
    cos_a, sin_a = jnp.cos(ang_a), jnp.sin(ang_a)
    ang_i = fi_ref[...] * posf
    cos_i, sin_i = jnp.cos(ang_i), jnp.sin(ang_i)

    q_scale = HEAD_DIM ** -0.5 * LOG2E
    for h in range(N_HEADS):
        b = h * HEAD_DIM
        x1 = qt_ref[b:b + ha, :].astype(_F32)
        x2 = qt_ref[b + ha:b + 2 * ha, :].astype(_F32)
        qto_ref[b:b + ha, :] = ((x1 * cos_a - x2 * sin_a) * q_scale).astype(_BF16)
        qto_ref[b + ha:b + 2 * ha, :] = ((x2 * cos_a + x1 * sin_a) * q_scale).astype(_BF16)
        rest = qt_ref[b + 2 * ha:b + HEAD_DIM, :].astype(_F32)
        qto_ref[b + 2 * ha:b + HEAD_DIM, :] = (rest * q_scale).astype(_BF16)

    for h in range(IDX_HEADS):
        b = h * IDX_DIM
        x1 = iqt_ref[b:b + hi_, :]
        x2 = iqt_ref[b + hi_:b + 2 * hi_, :]
        y = jnp.concatenate([x1 * cos_i - x2 * sin_i, x2 * cos_i + x1 * sin_i,
                             iqt_ref[b + 2 * hi_:b + IDX_DIM, :]], axis=0) * (IDX_DIM ** -0.5)
        yh = y.astype(_BF16)
        yl = (y - yh.astype(_F32)).astype(_BF16)
        o = h * MXU_DIM
        qcat_ref[o:o + IDX_DIM, :] = yh
        qcat_ref[o + IDX_DIM:o + 2 * IDX_DIM, :] = yh
        qcat_ref[o + 2 * IDX_DIM:o + 3 * IDX_DIM, :] = yl
        qcat_ref[o + 3 * IDX_DIM:o + 4 * IDX_DIM, :] = yl

    one_a = jnp.ones((LANES - 2 * ha, tr), _F32)
    zero = lambda n: jnp.zeros((n, tr), _F32)
    c_a = jnp.concatenate([cos_a, cos_a, one_a], axis=0).T
    sn_a = jnp.concatenate([-sin_a, zero(LANES - ha)], axis=0).T
    sp_a = jnp.concatenate([zero(ha), sin_a, zero(LANES - 2 * ha)], axis=0).T
    one_i = jnp.ones((IDX_DIM - 2 * hi_, tr), _F32)
    c_i = jnp.concatenate([cos_i, cos_i, one_i] * 2, axis=0).T
    sn_i = jnp.concatenate([-sin_i, zero(IDX_DIM - hi_)] * 2, axis=0).T
    sp_i = jnp.concatenate([zero(hi_), sin_i, zero(IDX_DIM - 2 * hi_)] * 2, axis=0).T

    for h in range(N_HEADS):
        sl = slice(h * HEAD_DIM, (h + 1) * HEAD_DIM)
        xk = ak_ref[:, sl].astype(_F32)
        k_ref[:, sl] = _rope(xk, c_a, sn_a, sp_a, ha).astype(_BF16)

    lane = lax.broadcasted_iota(jnp.int32, (1, LANES), 1)
    first = lane < IDX_DIM
    kk = small_ref[:, 0:LANES]
    mu = jnp.mean(kk, axis=-1, keepdims=True)
    var = jnp.mean(jnp.square(kk - mu), axis=-1, keepdims=True)
    kn = (kk - mu) * lax.rsqrt(var + EPS) * lng_ref[...] + lnb_ref[...]
    kr = _rope(kn, c_i, sn_i, sp_i, hi_)
    khi = kr.astype(_BF16).astype(_F32)
    kpair = jnp.where(first, khi, kr - khi).astype(_BF16)
    kcat_ref[:, 0:LANES] = kpair
    kcat_ref[:, LANES:2 * LANES] = kpair

    g = small_ref[:, LANES:2 * LANES]
    w_s = g * (IDX_HEADS ** -0.5)
    beta = jax.nn.sigmoid(g)
    z = g + dtb_ref[...]
    softplus = jnp.maximum(z, 0.0) + jnp.log1p(jnp.exp(-jnp.abs(z)))
    gg = -jnp.exp(alog_ref[...]) * softplus
    rowc = lax.broadcasted_iota(jnp.int32, (tr, LANES), 0) % CHUNK
    fwd = gg
    rev = gg
    s = 1
    while s < CHUNK:
        fwd = fwd + jnp.where(rowc >= s, pltpu.roll(fwd, s, 0), 0.0)
        rev = rev + jnp.where(rowc < CHUNK - s, pltpu.roll(rev, tr - s, 0), 0.0)
        s *= 2
    glast = fwd + rev - gg
    out = jnp.where(lane < 8, w_s, jnp.where(lane < 16, beta, jnp.where(lane < 24, fwd, 0.0)))
    out = jnp.where((lane >= 24) & (lane < 32), pltpu.roll(glast, 8, 1), out)
    scal_ref[...] = out


def _prep(big, qt, iqt, small, pos_row, fa, fi, lng, lnb, alog, dtb):
    m = big.shape[0]
    tr = 512
    row = lambda i: (i, 0)
    col = lambda i: (0, i)
    const = lambda i: (0, 0)
    return pl.pallas_call(
        _prep_kernel,
        grid=(m // tr,),
        in_specs=[pl.BlockSpec((tr, D_MODEL), lambda i: (i, CB_AK)),
                  pl.BlockSpec((D_MODEL, tr), col),
                  pl.BlockSpec((IQ_W, tr), col),
                  pl.BlockSpec((tr, SMALL_W), row),
                  pl.BlockSpec((1, tr), col),
                  pl.BlockSpec((16, 1), const), pl.BlockSpec((8, 1), const),
                  pl.BlockSpec((1, LANES), const), pl.BlockSpec((1, LANES), const),
                  pl.BlockSpec((1, LANES), const), pl.BlockSpec((1, LANES), const)],
        out_specs=[pl.BlockSpec((D_MODEL, tr), col), pl.BlockSpec((tr, D_MODEL), row),
                   pl.BlockSpec((IDX_HEADS * MXU_DIM, tr), col), pl.BlockSpec((tr, MXU_DIM), row),
                   pl.BlockSpec((tr, LANES), row)],
        out_shape=[jax.ShapeDtypeStruct((D_MODEL, m), _BF16), jax.ShapeDtypeStruct((m, D_MODEL), _BF16),
                   jax.ShapeDtypeStruct((IDX_HEADS * MXU_DIM, m), _BF16),
                   jax.ShapeDtypeStruct((m, MXU_DIM), _BF16),
                   jax.ShapeDtypeStruct((m, LANES), _F32)],
        compiler_params=_cparams(("parallel",), 48),
        name="prep",
    )(big, qt, iqt, small, pos_row, fa, fi, lng, lnb, alog, dtb)


def _tile_loop(nk, body, init):
    def pair(i, c):
        return body(2 * i + 1, body(2 * i, c))
    c = lax.fori_loop(0, nk // 2, pair, init)
    return lax.cond(nk % 2 == 1, lambda c: body(nk - 1, c), lambda c: c, c)


def _dsa_kernel(topk, qcat_ref, kcat_ref, q_ref, k_ref, vt_ref, wt_ref, o_ref,
                keys_ref, hi16_ref, lo16_ref, bias_ref, s_ref, p_ref, thr_ref):
    qi = pl.program_id(1)
    nk = qi + 1
    kt_rows = lax.broadcasted_iota(jnp.int32, (TQ, TQ), 0)
    row_minus_col = kt_rows - lax.broadcasted_iota(jnp.int32, (TQ, TQ), 1)

    def score_tile(kt, carry):
        r0 = pl.multiple_of(kt * TQ, TQ)
        kc = kcat_ref[pl.ds(r0, TQ), :]
        acc = jnp.zeros((TQ, TQ), _F32)
        for h in range(IDX_HEADS):
            lg = _dot(kc, qcat_ref[h * MXU_DIM:(h + 1) * MXU_DIM, :])
            acc = acc + wt_ref[h:h + 1, :] * jnp.maximum(lg, 0.0)
        bits = pltpu.bitcast(acc + 0.0, jnp.int32)
        key = bits ^ ((bits >> 31) & 0x7FFFFFFF)
        key = jnp.where(row_minus_col <= (qi - kt) * TQ, key, INT_MIN)
        keys_ref[kt] = key
        hi16_ref[kt] = (key >> 16).astype(jnp.int16)
        lo16_ref[kt] = (key - 32768).astype(jnp.int16)
        return carry

    _tile_loop(nk, score_tile, 0)

    def search_static(n_tiles):
        def count16(ref, pred_fn):
            accs = [jnp.zeros((16, TQ), jnp.int16) for _ in range(COUNT_LANES)]
            for kt in range(n_tiles):
                m = jnp.where(pred_fn(ref[kt]), jnp.int16(1), jnp.int16(0))
                for r in range(TQ // 16):
                    accs[r % COUNT_LANES] = accs[r % COUNT_LANES] + m[r * 16:(r + 1) * 16]
            acc = functools.reduce(lambda a, b: a + b, accs)
            return jnp.sum(acc.astype(jnp.int32), axis=0, keepdims=True)

        def search16(ref, want):
            cnt0 = count16(ref, lambda kv: kv >= jnp.int16(0))
            ok0 = cnt0 >= want
            thr = jnp.where(ok0, 0, -32768)
            cnt_thr = jnp.where(ok0, cnt0, n_tiles * TQ)

            def bit_step(it, carry):
                thr, cnt_thr = carry
                cand = thr + lax.shift_left(jnp.int32(1), 14 - it)
                cand16 = cand.astype(jnp.int16)
                cnt = count16(ref, lambda kv: kv >= cand16)
                ok = cnt >= want
                return jnp.where(ok, cand, thr), jnp.where(ok, cnt, cnt_thr)

            return lax.fori_loop(0, 15, bit_step, (thr, cnt_thr))

        thr_hi, _ = search16(hi16_ref, topk)
        thr_hi16 = thr_hi.astype(jnp.int16)
        n_above = count16(hi16_ref, lambda kv: kv > thr_hi16)
        for kt in range(n_tiles):
            lo16_ref[kt] = jnp.where(hi16_ref[kt] == thr_hi16, lo16_ref[kt], jnp.int16(-32768))
        thr_lo, cnt_lo = search16(lo16_ref, topk - n_above)
        thr_lo16 = thr_lo.astype(jnp.int16)
        thr = thr_hi * 65536 + (thr_lo + 32768)
        tie_lane = jnp.where((n_above + cnt_lo > topk) & (thr > INT_MIN), 1, 0)
        thr_ref[0:1, :] = thr
        thr_ref[1:2, :] = tie_lane

        @pl.when(jnp.max(tie_lane) > 0)
        def _():
            need = topk - (n_above + count16(lo16_ref, lambda kv: kv > thr_lo16))
            for kt in range(n_tiles):
                neg_pos = (-(kt_rows + kt * TQ)).astype(jnp.int16)
                at_thr = jnp.where(hi16_ref[kt] == thr_hi16, neg_pos, jnp.int16(-32768))
                lo16_ref[kt] = jnp.where(lo16_ref[kt] == thr_lo16, at_thr, jnp.int16(-32768))
            thr_pos, _ = search16(lo16_ref, need)
            thr_ref[2:3, :] = jnp.where(thr > INT_MIN, -thr_pos, -1)

    assert k_ref.shape[0] <= 32768, "positions are searched as int16"
    for n_tiles in range(1, keys_ref.shape[0] + 1):
        pl.when(nk == n_tiles)(functools.partial(search_static, n_tiles))
    thr = thr_ref[0:1, :]
    tie = jnp.max(thr_ref[1:2, :]) > 0

    @pl.when(jnp.logical_not(tie))
    def _():
        def body(kt, carry):
            kv = keys_ref[kt]
            bias_ref[kt] = jnp.where(kv >= thr_sel, 0.0, NEG_BIG)
            return carry
        thr_sel = jnp.maximum(thr, INT_MIN + 1)
        _tile_loop(nk, body, 0)

    @pl.when(tie)
    def _():
        last_pos = thr_ref[2:3, :]

        def body(kt, carry):
            kv = keys_ref[kt]
            at_thr = jnp.where((kt_rows + kt * TQ) <= last_pos, 0.0, NEG_BIG)
            bias_ref[kt] = jnp.where(kv > thr, 0.0, jnp.where(kv == thr, at_thr, NEG_BIG))
            return carry
        _tile_loop(nk, body, 0)

    def zero_tail(kt, carry):
        r0 = pl.multiple_of(kt * TQ, TQ)
        for i in range(HEAD_GROUP):
            p_ref[i, pl.ds(r0, TQ), :] = jnp.zeros((TQ, TQ), _BF16)
        return carry

    lax.fori_loop(nk, pl.num_programs(1), zero_tail, 0)
    for hg in range(N_HEADS // HEAD_GROUP):
        hs = [hg * HEAD_GROUP + i for i in range(HEAD_GROUP)]
        sls = [slice(h * HEAD_DIM, (h + 1) * HEAD_DIM) for h in hs]
        qhs = [q_ref[sl, :] for sl in sls]

        def s_tile(kt, mxs):
            r0 = pl.multiple_of(kt * TQ, TQ)
            bias = bias_ref[kt]
            out = []
            for i in range(HEAD_GROUP):
                s = _dot(k_ref[pl.ds(r0, TQ), sls[i]], qhs[i]) + bias
                s_ref[i, kt] = s
                out.append(jnp.maximum(mxs[i], jnp.max(s.reshape(TQ // 8, 8, TQ), axis=0)))
            return tuple(out)

        mxs = _tile_loop(nk, s_tile,
                            tuple(jnp.full((8, TQ), NEG_BIG, _F32) for _ in range(HEAD_GROUP)))
        mxs = [jnp.max(m, axis=0, keepdims=True) for m in mxs]

        def p_tile(kt, ls):
            r0 = pl.multiple_of(kt * TQ, TQ)
            out = []
            for i in range(HEAD_GROUP):
                p = jnp.exp2(s_ref[i, kt] - mxs[i])
                out.append(ls[i] + jnp.sum(p.reshape(TQ // 8, 8, TQ), axis=0))
                p_ref[i, pl.ds(r0, TQ), :] = p.astype(_BF16)
            return tuple(out)

        ls = lax.fori_loop(0, nk, p_tile,
                           tuple(jnp.zeros((8, TQ), _F32) for _ in range(HEAD_GROUP)))
        for i in range(HEAD_GROUP):
            l = jnp.sum(ls[i], axis=0, keepdims=True)
            acc = _dot(vt_ref[sls[i], :], p_ref[i])
            o_ref[:, sls[i]] = (acc / l).T.astype(_BF16)


def _dsa(qcat_t, kcat, q_t, k_r, vt, wt, topk):
    bsz, seq, _ = k_r.shape
    nq = seq // TQ
    return pl.pallas_call(
        functools.partial(_dsa_kernel, topk),
        grid=(bsz, nq),
        in_specs=[pl.BlockSpec((IDX_HEADS * MXU_DIM, TQ), lambda b, i: (0, b * nq + i)),
                  pl.BlockSpec((None, seq, MXU_DIM), lambda b, i: (b, 0, 0)),
                  pl.BlockSpec((D_MODEL, TQ), lambda b, i: (0, b * nq + i)),
                  pl.BlockSpec((None, seq, D_MODEL), lambda b, i: (b, 0, 0)),
                  pl.BlockSpec((D_MODEL, seq), lambda b, i: (0, b)),
                  pl.BlockSpec((None, IDX_HEADS, TQ), lambda b, i: (b, 0, i))],
        out_specs=pl.BlockSpec((None, TQ, D_MODEL), lambda b, i: (b, i, 0)),
        out_shape=jax.ShapeDtypeStruct((bsz, seq, D_MODEL), _BF16),
        scratch_shapes=[pltpu.VMEM((nq, TQ, TQ), jnp.int32),
                        pltpu.VMEM((nq, TQ, TQ), jnp.int16),
                        pltpu.VMEM((nq, TQ, TQ), jnp.int16),
                        pltpu.VMEM((nq, TQ, TQ), _F32),
                        pltpu.VMEM((HEAD_GROUP, nq, TQ, TQ), _F32),
                        pltpu.VMEM((HEAD_GROUP, seq, TQ), _BF16),
                        pltpu.VMEM((8, TQ), jnp.int32)],
        compiler_params=_cparams(("parallel", "arbitrary"), 48),
        name="dsa",
    )(qcat_t, kcat, q_t, k_r, vt, wt)


def _dnprep_kernel(x_ref, w_ref, o_ref):
    j = pl.program_id(1)
    seq, cw = x_ref.shape
    rt = 256
    halo = 16
    ri = lax.broadcasted_iota(jnp.int32, (rt, rt), 0)
    ci = lax.broadcasted_iota(jnp.int32, (rt, rt), 1)
    hr = lax.broadcasted_iota(jnp.int32, (halo, halo), 0)
    hc = lax.broadcasted_iota(jnp.int32, (halo, halo), 1)
    shifts = [jnp.where(ri - ci == d, 1.0, 0.0).astype(_BF16) for d in range(1, CONV_WIDTH)]
    carries = [jnp.where(hc - hr == halo - d, 1.0, 0.0).astype(_BF16) for d in range(1, CONV_WIDTH)]
    qk_scale = jnp.where(j < 2, HEAD_DIM ** -0.5, 1.0)
    is_qk = j < 4
    for r in range(seq // rt):
        xb = x_ref[r * rt:(r + 1) * rt, :]
        acc = w_ref[CONV_WIDTH - 1:CONV_WIDTH, :] * xb.astype(_F32)
        for d in range(1, CONV_WIDTH):
            acc = acc + w_ref[CONV_WIDTH - 1 - d:CONV_WIDTH - d, :] * _dot(shifts[d - 1], xb)
        if r > 0:
            prev = x_ref[r * rt - halo:r * rt, :]
            head = acc[:halo]
            for d in range(1, CONV_WIDTH):
                head = head + w_ref[CONV_WIDTH - 1 - d:CONV_WIDTH - d, :] * _dot(carries[d - 1], prev)
            acc = jnp.concatenate([head, acc[halo:]], axis=0)
        y = _silu(acc)
        for h in range(cw // HEAD_DIM):
            sl = slice(h * HEAD_DIM, (h + 1) * HEAD_DIM)
            yh = y[:, sl]
            nrm = lax.rsqrt(jnp.sum(yh * yh, axis=-1, keepdims=True) + EPS) * qk_scale
            o_ref[r * rt:(r + 1) * rt, sl] = (yh * jnp.where(is_qk, nrm, 1.0)).astype(_BF16)


def _dnprep(big3, conv_w):
    bsz, seq, _ = big3.shape
    cw = 512
    nblk = 3 * D_MODEL // cw
    cb = CB_BQ * D_MODEL // cw
    return pl.pallas_call(
        _dnprep_kernel,
        grid=(bsz, nblk),
        in_specs=[pl.BlockSpec((None, seq, cw), lambda b, j: (b, 0, cb + j)),
                  pl.BlockSpec((CONV_WIDTH, cw), lambda b, j: (0, j))],
        out_specs=pl.BlockSpec((None, seq, cw), lambda b, j: (b, 0, j)),
        out_shape=jax.ShapeDtypeStruct((bsz, seq, 3 * D_MODEL), _BF16),
        compiler_params=_cparams(("parallel", "arbitrary"), 48),
        name="dnprep",
    )(big3, conv_w)


N_LEVELS = 5
M_INCL, M_STRICT, M_PAIR, M_EYE, M_OFF0 = 0, 1, 2, 3, 4


def _delta_kernel(q_ref, k_ref, v_ref, z_ref, scal_ref, scalt_ref, gain_ref, o_ref,
                  state_ref, vnew_ref, mask_ref, lbf_ref, tbf_ref, pbf_ref, rhs_ref,
                  intra_ref, u_ref, wq_ref, kdt_ref):
    g_idx = pl.program_id(1)
    n_chunks = GROUP // CHUNK
    n_sub = GROUP // SUB
    per_sub = SUB // CHUNK
    units = [(h, b) for h in range(N_HEADS) for b in range(n_sub)]

    @pl.when(g_idx == 0)
    def _():
        state_ref[...] = jnp.zeros_like(state_ref)
        vnew_ref[...] = jnp.zeros_like(vnew_ref)
        ri = lax.broadcasted_iota(jnp.int32, (SUB, SUB), 0)
        ci = lax.broadcasted_iota(jnp.int32, (SUB, SUB), 1)
        same = (ri // CHUNK) == (ci // CHUNK)
        mask_ref[M_INCL] = jnp.where(same & (ri >= ci), 0.0, -jnp.inf)
        mask_ref[M_STRICT] = jnp.where(same & (ri > ci), 1.0, 0.0)
        mask_ref[M_PAIR] = jnp.where((ri // 2) == (ci // 2), 1.0, 0.0)
        mask_ref[M_EYE] = jnp.where(ri == ci, 1.0, 0.0)
        for lv in range(N_LEVELS):
            blk = 4 << lv
            off = ((ri // blk) == (ci // blk)) & ((ri // (blk // 2)) != (ci // (blk // 2)))
            mask_ref[M_OFF0 + lv] = jnp.where(off, 1.0, 0.0)

    for h, b in units:
        sl = slice(h * HEAD_DIM, (h + 1) * HEAD_DIM)
        rb = slice(b * SUB, (b + 1) * SUB)
        q = q_ref[rb, sl].astype(_F32)
        k_bf = k_ref[rb, sl]
        k = k_bf.astype(_F32)
        v = v_ref[rb, sl].astype(_F32)
        beta_c = scal_ref[rb, 8 + h:9 + h]
        gc_c = scal_ref[rb, 16 + h:17 + h]
        gl_c = scal_ref[rb, 24 + h:25 + h]
        gc_r = scalt_ref[16 + h:17 + h, rb]
        decay = jnp.exp((gc_c - gc_r) + mask_ref[M_INCL])
        kb = k * beta_c
        lmat = _dot_nt(kb.astype(_BF16), k_bf) * decay * mask_ref[M_STRICT]
        lbf_ref[h, b] = lmat.astype(_BF16)
        tbf_ref[h, b] = (mask_ref[M_EYE] - lmat * mask_ref[M_PAIR]).astype(_BF16)
        intra_ref[h, b] = (_dot_nt(q.astype(_BF16), k_bf) * decay).astype(_BF16)
        egc = jnp.exp(gc_c)
        rhs_ref[h, rb, 0:HEAD_DIM] = (v * beta_c).astype(_BF16)
        rhs_ref[h, rb, HEAD_DIM:2 * HEAD_DIM] = (kb * egc).astype(_BF16)
        qg = (q * egc).astype(_BF16)
        for c in range(per_sub):
            wq_ref[h, b * per_sub + c, CHUNK:2 * CHUNK, :] = qg[c * CHUNK:(c + 1) * CHUNK]
        kdt_ref[h, b] = (k * jnp.exp(gl_c - gc_c)).T.astype(_BF16)

    for lv in range(N_LEVELS):
        for h, b in units:
            p = _dot(lbf_ref[h, b], tbf_ref[h, b]) * mask_ref[M_OFF0 + lv]
            pbf_ref[h, b] = p.astype(_BF16)
        for h, b in units:
            t = tbf_ref[h, b]
            tbf_ref[h, b] = t - _dot(t, pbf_ref[h, b]).astype(_BF16)

    for h, b in units:
        rb = slice(b * SUB, (b + 1) * SUB)
        uw = _dot(tbf_ref[h, b], rhs_ref[h, rb, :])
        u_ref[h, rb, :] = uw[:, :HEAD_DIM]
        w_bf = uw[:, HEAD_DIM:].astype(_BF16)
        for c in range(per_sub):
            wq_ref[h, b * per_sub + c, 0:CHUNK, :] = w_bf[c * CHUNK:(c + 1) * CHUNK]

    lane_chunk = lax.broadcasted_iota(jnp.int32, (1, SUB), 1) // CHUNK
    for n in range(n_chunks):
        b, c = divmod(n, per_sub)
        rs = slice(n * CHUNK, (n + 1) * CHUNK)
        rc = slice(c * CHUNK, (c + 1) * CHUNK)
        for h in range(N_HEADS):
            ws = _dot(wq_ref[h, n], state_ref[h].astype(_BF16))
            vnew_ref[h, b, rc, :] = (u_ref[h, rs, :] - ws[:CHUNK]).astype(_BF16)
            u_ref[h, rs, :] = ws[CHUNK:]
        for h in range(N_HEADS):
            sl = slice(h * HEAD_DIM, (h + 1) * HEAD_DIM)
            vn = vnew_ref[h, b]
            o_n = u_ref[h, rs, :] + _dot(intra_ref[h, b, rc, :], vn)
            kd = jnp.where(lane_chunk == c, kdt_ref[h, b], jnp.zeros((), _BF16))
            gl = scal_ref[n * CHUNK:n * CHUNK + 1, 24 + h:25 + h]
            state_ref[h] = state_ref[h] * jnp.exp(gl) + _dot(kd, vn)
            ms = jnp.mean(o_n * o_n, axis=-1, keepdims=True)
            zn = z_ref[rs, sl].astype(_F32)
            o_ref[rs, sl] = (o_n * lax.rsqrt(ms + EPS) * gain_ref[...] * _silu(zn)).astype(_BF16)


def _delta(qkv, big3, scal3, scalt, gain):
    bsz, seq, _ = qkv.shape
    ng = seq // GROUP
    n_sub = GROUP // SUB
    hm = (N_HEADS, n_sub, SUB, SUB)
    return pl.pallas_call(
        _delta_kernel,
        grid=(bsz, ng),
        in_specs=[pl.BlockSpec((None, GROUP, D_MODEL), lambda b, g: (b, g, 0)),
                  pl.BlockSpec((None, GROUP, D_MODEL), lambda b, g: (b, g, 1)),
                  pl.BlockSpec((None, GROUP, D_MODEL), lambda b, g: (b, g, 2)),
                  pl.BlockSpec((None, GROUP, D_MODEL), lambda b, g: (b, g, CB_BZ)),
                  pl.BlockSpec((None, GROUP, LANES), lambda b, g: (b, g, 0)),
                  pl.BlockSpec((None, 32, GROUP), lambda b, g: (b, 0, g)),
                  pl.BlockSpec((1, HEAD_DIM), lambda b, g: (0, 0))],
        out_specs=pl.BlockSpec((None, GROUP, D_MODEL), lambda b, g: (b, g, 0)),
        out_shape=jax.ShapeDtypeStruct((bsz, seq, D_MODEL), _BF16),
        scratch_shapes=[pltpu.VMEM((N_HEADS, HEAD_DIM, HEAD_DIM), _F32),
                        pltpu.VMEM((N_HEADS, n_sub, SUB, HEAD_DIM), _BF16),
                        pltpu.VMEM((M_OFF0 + N_LEVELS, SUB, SUB), _F32),
                        pltpu.VMEM(hm, _BF16),
                        pltpu.VMEM(hm, _BF16),
                        pltpu.VMEM(hm, _BF16),
                        pltpu.VMEM((N_HEADS, GROUP, 2 * HEAD_DIM), _BF16),
                        pltpu.VMEM(hm, _BF16),
                        pltpu.VMEM((N_HEADS, GROUP, HEAD_DIM), _F32),
                        pltpu.VMEM((N_HEADS, GROUP // CHUNK, 2 * CHUNK, HEAD_DIM), _BF16),
                        pltpu.VMEM((N_HEADS, n_sub, HEAD_DIM, SUB), _BF16)],
        compiler_params=_cparams(("parallel", "arbitrary"), 48),
        name="delta",
    )(qkv, qkv, qkv, big3, scal3, scalt, gain)


def _out_kernel(oa_ref, az_ref, ob_ref, ga_ref, gb_ref, x_ref, mod_ref, gain_ref,
                wa_ref, wb_ref, wo_ref, o_ref):
    za = (oa_ref[...].astype(_F32) * _silu(az_ref[...].astype(_F32))).astype(_BF16)
    ya = _dot(za, wa_ref[...])
    yb = _dot(ob_ref[...], wb_ref[...])
    y = jax.nn.sigmoid(ga_ref[...].astype(_F32)) * ya + jax.nn.sigmoid(gb_ref[...].astype(_F32)) * yb
    out = _dot(y.astype(_BF16), wo_ref[...])
    ms = jnp.mean(out * out, axis=-1, keepdims=True)
    gate = mod_ref[0, :, 2 * D_MODEL:3 * D_MODEL]
    o_ref[...] = x_ref[...] + gate * (out * lax.rsqrt(ms + EPS) * gain_ref[...])


def _out(oa2, big, ob2, x2, mod3, post_gain, wa, wb, wo, seq):
    m = x2.shape[0]
    tm = 512
    per_b = seq // tm
    row = lambda i: (i, 0)
    const = lambda i: (0, 0)
    return pl.pallas_call(
        _out_kernel,
        grid=(m // tm,),
        in_specs=[pl.BlockSpec((tm, D_MODEL), row),
                  pl.BlockSpec((tm, D_MODEL), lambda i: (i, CB_AZ)),
                  pl.BlockSpec((tm, D_MODEL), row),
                  pl.BlockSpec((tm, D_MODEL), lambda i: (i, CB_GA)),
                  pl.BlockSpec((tm, D_MODEL), lambda i: (i, CB_GB)),
                  pl.BlockSpec((tm, D_MODEL), row),
                  pl.BlockSpec((1, 1, 3 * D_MODEL), lambda i: (i // per_b, 0, 0)),
                  pl.BlockSpec((1, D_MODEL), const),
                  pl.BlockSpec((D_MODEL, D_MODEL), const),
                  pl.BlockSpec((D_MODEL, D_MODEL), const),
                  pl.BlockSpec((D_MODEL, D_MODEL), const)],
        out_specs=pl.BlockSpec((tm, D_MODEL), row),
        out_shape=jax.ShapeDtypeStruct((m, D_MODEL), _F32),
        compiler_params=_cparams(("parallel",), 48),
        name="out",
    )(oa2, big, ob2, big, big, x2, mod3, post_gain, wa, wb, wo)


def _pad_lanes(v, start):
    return jnp.zeros((1, LANES), _F32).at[0, start:start + v.shape[0]].set(v)


def _layer(x, c, positions, w_ada, b_ada, pre_gain, post_gain, w_in, ln_gain, ln_bias,
           conv_w, a_log, dt_bias, dn_gain, w_a_out, w_b_out, w_o):
    bsz, seq, d = x.shape
    m = bsz * seq
    topk = min(TOPK_MAX, seq // 4)

    pts = np.cumsum([0, 1024, 1024, 1024, 1024, 512, 64, 8, 1024, 1024, 1024, 1024, 8, 8, 1024, 1024])
    col = lambda i: w_in[:, pts[i]:pts[i + 1]]
    (aq, ak, av, az, iq, ik, iw, bq, bk, bv, bz, bbeta, ba, ga, gb) = [col(i) for i in range(15)]
    w_big = jnp.concatenate([ak, az, bq, bk, bv, bz, ga, gb], axis=1).astype(_BF16)
    w_qvt = jnp.concatenate([aq, av], axis=1).T.astype(_BF16)
    w_small = jnp.concatenate(
        [ik, ik, iw, bbeta, ba, jnp.zeros((d, SMALL_W - 152), _F32)], axis=1)
    ws_hi = w_small.astype(_BF16)
    ws_lo = (w_small - ws_hi.astype(_F32)).astype(_BF16)
    iq_t = iq.T
    wiq_hi = iq_t.astype(_BF16)
    wiq_lo = (iq_t - wiq_hi.astype(_F32)).astype(_BF16)

    mod = _ada(c, w_ada, b_ada)
    mod3 = mod.reshape(bsz, 1, 3 * d)
    x2 = x.reshape(m, d)
    big, small, qt, vt, iqt = _proj(x2, mod3, pre_gain.reshape(1, d), w_big, ws_hi, ws_lo,
                                    w_qvt, wiq_hi, wiq_lo, seq)

    rot_a = HEAD_DIM // ROPE_FRACTION
    rot_i = IDX_DIM // ROPE_FRACTION
    invf_a = ROPE_THETA ** (-(jnp.arange(rot_a // 2, dtype=_F32) * 2.0 / rot_a))
    invf_i = ROPE_THETA ** (-(jnp.arange(rot_i // 2, dtype=_F32) * 2.0 / rot_i))
    lng = jnp.concatenate([ln_gain, ln_gain]).reshape(1, LANES)
    lnb = jnp.concatenate([ln_bias, ln_bias]).reshape(1, LANES)
    q_t, k_r, qcat_t, kcat, scal = _prep(big, qt, iqt, small, positions.reshape(1, m),
                                         invf_a.reshape(-1, 1), invf_i.reshape(-1, 1), lng, lnb,
                                         _pad_lanes(a_log, 16), _pad_lanes(dt_bias, 16))

    scal3 = scal.reshape(bsz, seq, LANES)
    scalt = jnp.transpose(scal3[:, :, :32], (0, 2, 1))
    o_a = _dsa(qcat_t, kcat.reshape(bsz, seq, -1), q_t, k_r.reshape(bsz, seq, d),
               vt, scalt[:, 0:IDX_HEADS, :], topk)

    big3 = big.reshape(bsz, seq, N_BIG)
    qkv = _dnprep(big3, conv_w)
    o_b = _delta(qkv, big3, scal3, scalt, dn_gain.reshape(1, HEAD_DIM))

    y = _out(o_a.reshape(m, d), big, o_b.reshape(m, d), x2, mod3, post_gain.reshape(1, d),
             w_a_out.astype(_BF16), w_b_out.astype(_BF16), w_o.astype(_BF16), seq)
    return y.reshape(bsz, seq, d)


def kernel(x, c, positions, w_ada, b_ada, pre_norm_gain, post_norm_gain, w_in, idx_k_ln_gain,
           idx_k_ln_bias, dn_conv_w, dn_a_log, dn_dt_bias, dn_norm_gain, w_a_out, w_b_out, w_o):
    for layer in range(w_ada.shape[0]):
        x = _layer(x, c, positions, w_ada[layer], b_ada[layer], pre_norm_gain[layer],
                   post_norm_gain[layer], w_in[layer], idx_k_ln_gain[layer], idx_k_ln_bias[layer],
                   dn_conv_w[layer], dn_a_log[layer], dn_dt_bias[layer], dn_norm_gain[layer],
                   w_a_out[layer], w_b_out[layer], w_o[layer])
    return x
```

```python
import functools

import numpy as np
import jax
import jax.numpy as jnp
from jax import lax
from jax.experimental import pallas as pl
from jax.experimental.pallas import tpu as pltpu

D_MODEL = 1024
N_HEADS = 8
HEAD_DIM = 128
IDX_HEADS = 8
IDX_DIM = 64
TOPK_MAX = 256
CONV_WIDTH = 4
CHUNK = 64
ROPE_THETA = 500000.0
ROPE_FRACTION = 4
EPS = 1e-6

LANES = 128
MXU_DIM = 256
TQ = MXU_DIM
GROUP = MXU_DIM
SUB = 128
SMALL_W = 256
IQ_W = IDX_HEADS * IDX_DIM
NEG_BIG = -1e30
HEAD_GROUP = 8
LOG2E = 1.4426950408889634
COUNT_LANES = 8
INT_MIN = -2147483648

CB_AK, CB_AZ, CB_BQ, CB_BK, CB_BV, CB_BZ, CB_GA, CB_GB = range(8)
N_BIG = 8 * D_MODEL

_F32 = jnp.float32
_BF16 = jnp.bfloat16


def _cparams(sem, vmem_mb):
    return pltpu.CompilerParams(dimension_semantics=sem, vmem_limit_bytes=vmem_mb * 1024 * 1024)


def _split2(a):
    hi = a.astype(_BF16)
    lo = (a - hi.astype(_F32)).astype(_BF16)
    return hi, lo


def _dot(a, b):
    return jnp.dot(a, b, preferred_element_type=_F32)


def _dot_nt(a, b):
    return lax.dot_general(a, b, (((1,), (1,)), ((), ())), preferred_element_type=_F32)


def _silu(x):
    return x * jax.nn.sigmoid(x)


def _ada_kernel(c_ref, w_ref, b_ref, o_ref):
    c1, c2 = _split2(c_ref[...])
    c3 = (c_ref[...] - c1.astype(_F32) - c2.astype(_F32)).astype(_BF16)
    w = w_ref[...]
    w1, w2 = _split2(w)
    w3 = (w - w1.astype(_F32) - w2.astype(_F32)).astype(_BF16)
    acc = _dot(c1, w3) + _dot(c2, w2) + _dot(c3, w1)
    acc = acc + _dot(c1, w2) + _dot(c2, w1)
    acc = acc + _dot(c1, w1)
    o_ref[...] = acc + b_ref[...]


def _ada(c, w_ada, b_ada):
    bsz = c.shape[0]
    n = w_ada.shape[1]
    tn = 512
    return pl.pallas_call(
        _ada_kernel,
        grid=(n // tn,),
        in_specs=[pl.BlockSpec((bsz, D_MODEL), lambda j: (0, 0)),
                  pl.BlockSpec((D_MODEL, tn), lambda j: (0, j)),
                  pl.BlockSpec((1, tn), lambda j: (0, j))],
        out_specs=pl.BlockSpec((bsz, tn), lambda j: (0, j)),
        out_shape=jax.ShapeDtypeStruct((bsz, n), _F32),
        compiler_params=_cparams(("arbitrary",), 32),
        name="ada",
    )(c, w_ada, b_ada.reshape(1, n))


def _proj_kernel(x_ref, mod_ref, gain_ref, wbig_ref, wsh_ref, wsl_ref, wqvt_ref, wiqh_ref, wiql_ref,
                 big_ref, small_ref, qt_ref, vt_ref, iqt_ref, h_ref):
    j = pl.program_id(1)

    @pl.when(j == 0)
    def _():
        x = x_ref[...]
        ms = jnp.mean(x * x, axis=-1, keepdims=True)
        y = x * lax.rsqrt(ms + EPS) * gain_ref[...]
        shift = mod_ref[0, :, 0:D_MODEL]
        scale = mod_ref[0, :, D_MODEL:2 * D_MODEL]
        h = y * (1.0 + scale) + shift
        hi, lo = _split2(h)
        h_ref[...] = hi
        small_ref[...] = (_dot(hi, wsl_ref[...]) + _dot(lo, wsh_ref[...])) + _dot(hi, wsh_ref[...])
        qv = _dot_nt(wqvt_ref[...], hi).astype(_BF16)
        qt_ref[...] = qv[:D_MODEL]
        vt_ref[...] = qv[D_MODEL:]
        iqt_ref[...] = (_dot_nt(wiql_ref[...], hi) + _dot_nt(wiqh_ref[...], lo)) + _dot_nt(wiqh_ref[...], hi)

    big_ref[...] = _dot(h_ref[...], wbig_ref[...]).astype(_BF16)


def _proj(x2, mod3, pre_gain, w_big, ws_hi, ws_lo, w_qvt, wiq_hi, wiq_lo, seq):
    m = x2.shape[0]
    tm, tn = min(1024, seq), 1024
    per_b = seq // tm
    const = lambda i, j: (0, 0)
    colblk = lambda i, j: (0, i)
    once = pl.Buffered(1)
    return pl.pallas_call(
        _proj_kernel,
        grid=(m // tm, N_BIG // tn),
        in_specs=[pl.BlockSpec((tm, D_MODEL), lambda i, j: (i, 0)),
                  pl.BlockSpec((1, 1, 3 * D_MODEL), lambda i, j: (i // per_b, 0, 0)),
                  pl.BlockSpec((1, D_MODEL), const),
                  pl.BlockSpec((D_MODEL, tn), lambda i, j: (0, j)),
                  pl.BlockSpec((D_MODEL, SMALL_W), const, pipeline_mode=once),
                  pl.BlockSpec((D_MODEL, SMALL_W), const, pipeline_mode=once),
                  pl.BlockSpec((2 * D_MODEL, D_MODEL), const, pipeline_mode=once),
                  pl.BlockSpec((IQ_W, D_MODEL), const, pipeline_mode=once),
                  pl.BlockSpec((IQ_W, D_MODEL), const, pipeline_mode=once)],
        out_specs=[pl.BlockSpec((tm, tn), lambda i, j: (i, j)),
                   pl.BlockSpec((tm, SMALL_W), lambda i, j: (i, 0)),
                   pl.BlockSpec((D_MODEL, tm), colblk),
                   pl.BlockSpec((D_MODEL, tm), colblk),
                   pl.BlockSpec((IQ_W, tm), colblk)],
        out_shape=[jax.ShapeDtypeStruct((m, N_BIG), _BF16),
                   jax.ShapeDtypeStruct((m, SMALL_W), _F32),
                   jax.ShapeDtypeStruct((D_MODEL, m), _BF16),
                   jax.ShapeDtypeStruct((D_MODEL, m), _BF16),
                   jax.ShapeDtypeStruct((IQ_W, m), _F32)],
        scratch_shapes=[pltpu.VMEM((tm, D_MODEL), _BF16)],
        compiler_params=_cparams(("parallel", "arbitrary"), 58),
        name="proj",
    )(x2, mod3, pre_gain, w_big, ws_hi, ws_lo, w_qvt, wiq_hi, wiq_lo)


def _rope(x, c, sn, sp, half):
    return x * c + pltpu.roll(x, LANES - half, 1) * sn + pltpu.roll(x, half, 1) * sp


def _prep_kernel(ak_ref, qt_ref, iqt_ref, small_ref, pos_ref, fa_ref, fi_ref, lng_ref, lnb_ref,
                 alog_ref, dtb_ref, qto_ref, k_ref, qcat_ref, kcat_ref, scal_ref):
    tr = ak_ref.shape[0]
    ha, hi_ = 16, 8
    posf = pos_ref[...].astype(_F32)
    ang_a = fa_ref[...] * posf
    cos_a, sin_a = jnp.cos(ang_a), jnp.sin(ang_a)
    ang_i = fi_ref[...] * posf
    cos_i, sin_i = jnp.cos(ang_i), jnp.sin(ang_i)

    q_scale = HEAD_DIM ** -0.5 * LOG2E
    for h in range(N_HEADS):
        b = h * HEAD_DIM
        x1 = qt_ref[b:b + ha, :].astype(_F32)
        x2 = qt_ref[b + ha:b + 2 * ha, :].astype(_F32)
        qto_ref[b:b + ha, :] = ((x1 * cos_a - x2 * sin_a) * q_scale).astype(_BF16)
        qto_ref[b + ha:b + 2 * ha, :] = ((x2 * cos_a + x1 * sin_a) * q_scale).astype(_BF16)
        rest = qt_ref[b + 2 * ha:b + HEAD_DIM, :].astype(_F32)
        qto_ref[b + 2 * ha:b + HEAD_DIM, :] = (rest * q_scale).astype(_BF16)

    for h in range(IDX_HEADS):
        b = h * IDX_DIM
        x1 = iqt_ref[b:b + hi_, :]
        x2 = iqt_ref[b + hi_:b + 2 * hi_, :]
        y = jnp.concatenate([x1 * cos_i - x2 * sin_i, x2 * cos_i + x1 * sin_i,
                             iqt_ref[b + 2 * hi_:b + IDX_DIM, :]], axis=0) * (IDX_DIM ** -0.5)
        yh = y.astype(_BF16)
        yl = (y - yh.astype(_F32)).astype(_BF16)
        o = h * MXU_DIM
        qcat_ref[o:o + IDX_DIM, :] = yh
        qcat_ref[o + IDX_DIM:o + 2 * IDX_DIM, :] = yh
        qcat_ref[o + 2 * IDX_DIM:o + 3 * IDX_DIM, :] = yl
        qcat_ref[o + 3 * IDX_DIM:o + 4 * IDX_DIM, :] = yl

    one_a = jnp.ones((LANES - 2 * ha, tr), _F32)
    zero = lambda n: jnp.zeros((n, tr), _F32)
    c_a = jnp.concatenate([cos_a, cos_a, one_a], axis=0).T
    sn_a = jnp.concatenate([-sin_a, zero(LANES - ha)], axis=0).T
    sp_a = jnp.concatenate([zero(ha), sin_a, zero(LANES - 2 * ha)], axis=0).T
    one_i = jnp.ones((IDX_DIM - 2 * hi_, tr), _F32)
    c_i = jnp.concatenate([cos_i, cos_i, one_i] * 2, axis=0).T
    sn_i = jnp.concatenate([-sin_i, zero(IDX_DIM - hi_)] * 2, axis=0).T
    sp_i = jnp.concatenate([zero(hi_), sin_i, zero(IDX_DIM - 2 * hi_)] * 2, axis=0).T

    for h in range(N_HEADS):
        sl = slice(h * HEAD_DIM, (h + 1) * HEAD_DIM)
        xk = ak_ref[:, sl].astype(_F32)
        k_ref[:, sl] = _rope(xk, c_a, sn_a, sp_a, ha).astype(_BF16)

    lane = lax.broadcasted_iota(jnp.int32, (1, LANES), 1)
    first = lane < IDX_DIM
    kk = small_ref[:, 0:LANES]
    mu = jnp.mean(kk, axis=-1, keepdims=True)
    var = jnp.mean(jnp.square(kk - mu), axis=-1, keepdims=True)
    kn = (kk - mu) * lax.rsqrt(var + EPS) * lng_ref[...] + lnb_ref[...]
    kr = _rope(kn, c_i, sn_i, sp_i, hi_)
    khi = kr.astype(_BF16).astype(_F32)
    kpair = jnp.where(first, khi, kr - khi).astype(_BF16)
    kcat_ref[:, 0:LANES] = kpair
    kcat_ref[:, LANES:2 * LANES] = kpair

    g = small_ref[:, LANES:2 * LANES]
    w_s = g * (IDX_HEADS ** -0.5)
    beta = jax.nn.sigmoid(g)
    z = g + dtb_ref[...]
    softplus = jnp.maximum(z, 0.0) + jnp.log1p(jnp.exp(-jnp.abs(z)))
    gg = -jnp.exp(alog_ref[...]) * softplus
    rowc = lax.broadcasted_iota(jnp.int32, (tr, LANES), 0) % CHUNK
    fwd = gg
    rev = gg
    s = 1
    while s < CHUNK:
        fwd = fwd + jnp.where(rowc >= s, pltpu.roll(fwd, s, 0), 0.0)
        rev = rev + jnp.where(rowc < CHUNK - s, pltpu.roll(rev, tr - s, 0), 0.0)
        s *= 2
    glast = fwd + rev - gg
    out = jnp.where(lane < 8, w_s, jnp.where(lane < 16, beta, jnp.where(lane < 24, fwd, 0.0)))
    out = jnp.where((lane >= 24) & (lane < 32), pltpu.roll(glast, 8, 1), out)
    scal_ref[...] = out


def _prep(big, qt, iqt, small, pos_row, fa, fi, lng, lnb, alog, dtb):
    m = big.shape[0]
    tr = 512
    row = lambda i: (i, 0)
    col = lambda i: (0, i)
    const = lambda i: (0, 0)
    return pl.pallas_call(
        _prep_kernel,
        grid=(m // tr,),
        in_specs=[pl.BlockSpec((tr, D_MODEL), lambda i: (i, CB_AK)),
                  pl.BlockSpec((D_MODEL, tr), col),
                  pl.BlockSpec((IQ_W, tr), col),
                  pl.BlockSpec((tr, SMALL_W), row),
                  pl.BlockSpec((1, tr), col),
                  pl.BlockSpec((16, 1), const), pl.BlockSpec((8, 1), const),
                  pl.BlockSpec((1, LANES), const), pl.BlockSpec((1, LANES), const),
                  pl.BlockSpec((1, LANES), const), pl.BlockSpec((1, LANES), const)],
        out_specs=[pl.BlockSpec((D_MODEL, tr), col), pl.BlockSpec((tr, D_MODEL), row),
                   pl.BlockSpec((IDX_HEADS * MXU_DIM, tr), col), pl.BlockSpec((tr, MXU_DIM), row),
                   pl.BlockSpec((tr, LANES), row)],
        out_shape=[jax.ShapeDtypeStruct((D_MODEL, m), _BF16), jax.ShapeDtypeStruct((m, D_MODEL), _BF16),
                   jax.ShapeDtypeStruct((IDX_HEADS * MXU_DIM, m), _BF16),
                   jax.ShapeDtypeStruct((m, MXU_DIM), _BF16),
                   jax.ShapeDtypeStruct((m, LANES), _F32)],
        compiler_params=_cparams(("parallel",), 48),
        name="prep",
    )(big, qt, iqt, small, pos_row, fa, fi, lng, lnb, alog, dtb)


def _tile_loop(nk, body, init):
    def pair(i, c):
        return body(2 * i + 1, body(2 * i, c))
    c = lax.fori_loop(0, nk // 2, pair, init)
    return lax.cond(nk % 2 == 1, lambda c: body(nk - 1, c), lambda c: c, c)


def _dsa_kernel(topk, qcat_ref, kcat_ref, q_ref, k_ref, vt_ref, wt_ref, o_ref,
                keys_ref, hi16_ref, lo16_ref, bias_ref, s_ref, p_ref, thr_ref):
    qi = pl.program_id(1)
    nk = qi + 1
    kt_rows = lax.broadcasted_iota(jnp.int32, (TQ, TQ), 0)
    row_minus_col = kt_rows - lax.broadcasted_iota(jnp.int32, (TQ, TQ), 1)

    def score_tile(kt, carry):
        r0 = pl.multiple_of(kt * TQ, TQ)
        kc = kcat_ref[pl.ds(r0, TQ), :]
        acc = jnp.zeros((TQ, TQ), _F32)
        for h in range(IDX_HEADS):
            lg = _dot(kc, qcat_ref[h * MXU_DIM:(h + 1) * MXU_DIM, :])
            acc = acc + wt_ref[h:h + 1, :] * jnp.maximum(lg, 0.0)
        bits = pltpu.bitcast(acc + 0.0, jnp.int32)
        key = bits ^ ((bits >> 31) & 0x7FFFFFFF)
        key = jnp.where(row_minus_col <= (qi - kt) * TQ, key, INT_MIN)
        keys_ref[kt] = key
        hi16_ref[kt] = (key >> 16).astype(jnp.int16)
        lo16_ref[kt] = (key - 32768).astype(jnp.int16)
        return carry

    _tile_loop(nk, score_tile, 0)

    def search_static(n_tiles):
        def count16(ref, pred_fn):
            accs = [jnp.zeros((16, TQ), jnp.int16) for _ in range(COUNT_LANES)]
            for kt in range(n_tiles):
                m = jnp.where(pred_fn(ref[kt]), jnp.int16(1), jnp.int16(0))
                for r in range(TQ // 16):
                    accs[r % COUNT_LANES] = accs[r % COUNT_LANES] + m[r * 16:(r + 1) * 16]
            acc = functools.reduce(lambda a, b: a + b, accs)
            return jnp.sum(acc.astype(jnp.int32), axis=0, keepdims=True)

        def search16(ref, want, floor_bits=None):
            if floor_bits is None:
                cnt0 = count16(ref, lambda kv: kv >= jnp.int16(0))
                ok0 = cnt0 >= want
                thr = jnp.where(ok0, 0, -32768)
                cnt_thr = jnp.where(ok0, cnt0, n_tiles * TQ)
                n_bits = 15
            else:
                thr = jnp.full((1, TQ), -(1 << floor_bits), jnp.int32)
                cnt_thr = jnp.zeros((1, TQ), jnp.int32)
                n_bits = floor_bits

            def bit_step(it, carry):
                thr, cnt_thr = carry
                cand = thr + lax.shift_left(jnp.int32(1), n_bits - 1 - it)
                cand16 = cand.astype(jnp.int16)
                cnt = count16(ref, lambda kv: kv >= cand16)
                ok = cnt >= want
                return jnp.where(ok, cand, thr), jnp.where(ok, cnt, cnt_thr)

            return lax.fori_loop(0, n_bits, bit_step, (thr, cnt_thr))

        thr_hi, _ = search16(hi16_ref, topk)
        thr_hi16 = thr_hi.astype(jnp.int16)
        n_above = count16(hi16_ref, lambda kv: kv > thr_hi16)
        for kt in range(n_tiles):
            lo16_ref[kt] = jnp.where(hi16_ref[kt] == thr_hi16, lo16_ref[kt], jnp.int16(-32768))
        thr_lo, cnt_lo = search16(lo16_ref, topk - n_above)
        thr_lo16 = thr_lo.astype(jnp.int16)
        thr = thr_hi * 65536 + (thr_lo + 32768)
        tie_lane = jnp.where((n_above + cnt_lo > topk) & (thr > INT_MIN), 1, 0)
        thr_ref[0:1, :] = thr
        thr_ref[1:2, :] = tie_lane

        @pl.when(jnp.max(tie_lane) > 0)
        def _():
            need = topk - (n_above + count16(lo16_ref, lambda kv: kv > thr_lo16))
            for kt in range(n_tiles):
                neg_pos = (-1 - (kt_rows + kt * TQ)).astype(jnp.int16)
                at_thr = jnp.where(hi16_ref[kt] == thr_hi16, neg_pos, jnp.int16(-32768))
                lo16_ref[kt] = jnp.where(lo16_ref[kt] == thr_lo16, at_thr, jnp.int16(-32768))
            thr_pos, _ = search16(lo16_ref, need, floor_bits=pos_bits)
            thr_ref[2:3, :] = jnp.where(thr > INT_MIN, -1 - thr_pos, -1)

    pos_bits = max(1, (k_ref.shape[0] - 1).bit_length())
    assert pos_bits < 15, "positions are searched as negative int16 values"
    for n_tiles in range(1, keys_ref.shape[0] + 1):
        pl.when(nk == n_tiles)(functools.partial(search_static, n_tiles))
    thr = thr_ref[0:1, :]
    tie = jnp.max(thr_ref[1:2, :]) > 0

    @pl.when(jnp.logical_not(tie))
    def _():
        def body(kt, carry):
            kv = keys_ref[kt]
            bias_ref[kt] = jnp.where(kv >= thr_sel, 0.0, NEG_BIG)
            return carry
        thr_sel = jnp.maximum(thr, INT_MIN + 1)
        _tile_loop(nk, body, 0)

    @pl.when(tie)
    def _():
        last_pos = thr_ref[2:3, :]

        def body(kt, carry):
            kv = keys_ref[kt]
            at_thr = jnp.where((kt_rows + kt * TQ) <= last_pos, 0.0, NEG_BIG)
            bias_ref[kt] = jnp.where(kv > thr, 0.0, jnp.where(kv == thr, at_thr, NEG_BIG))
            return carry
        _tile_loop(nk, body, 0)

    def zero_tail(kt, carry):
        r0 = pl.multiple_of(kt * TQ, TQ)
        for i in range(HEAD_GROUP):
            p_ref[i, pl.ds(r0, TQ), :] = jnp.zeros((TQ, TQ), _BF16)
        return carry

    lax.fori_loop(nk, pl.num_programs(1), zero_tail, 0)
    for hg in range(N_HEADS // HEAD_GROUP):
        hs = [hg * HEAD_GROUP + i for i in range(HEAD_GROUP)]
        sls = [slice(h * HEAD_DIM, (h + 1) * HEAD_DIM) for h in hs]
        qhs = [q_ref[sl, :] for sl in sls]

        def s_tile(kt, mxs):
            r0 = pl.multiple_of(kt * TQ, TQ)
            bias = bias_ref[kt]
            out = []
            for i in range(HEAD_GROUP):
                s = _dot(k_ref[pl.ds(r0, TQ), sls[i]], qhs[i]) + bias
                s_ref[i, kt] = s
                out.append(jnp.maximum(mxs[i], jnp.max(s.reshape(TQ // 8, 8, TQ), axis=0)))
            return tuple(out)

        mxs = _tile_loop(nk, s_tile,
                            tuple(jnp.full((8, TQ), NEG_BIG, _F32) for _ in range(HEAD_GROUP)))
        mxs = [jnp.max(m, axis=0, keepdims=True) for m in mxs]

        def p_tile(kt, ls):
            r0 = pl.multiple_of(kt * TQ, TQ)
            out = []
            for i in range(HEAD_GROUP):
                p = jnp.exp2(s_ref[i, kt] - mxs[i])
                out.append(ls[i] + jnp.sum(p.reshape(TQ // 8, 8, TQ), axis=0))
                p_ref[i, pl.ds(r0, TQ), :] = p.astype(_BF16)
            return tuple(out)

        ls = lax.fori_loop(0, nk, p_tile,
                           tuple(jnp.zeros((8, TQ), _F32) for _ in range(HEAD_GROUP)))
        for i in range(HEAD_GROUP):
            l = jnp.sum(ls[i], axis=0, keepdims=True)
            acc = _dot(vt_ref[sls[i], :], p_ref[i])
            o_ref[:, sls[i]] = (acc / l).T.astype(_BF16)


def _dsa(qcat_t, kcat, q_t, k_r, vt, wt, topk):
    bsz, seq, _ = k_r.shape
    nq = seq // TQ
    return pl.pallas_call(
        functools.partial(_dsa_kernel, topk),
        grid=(bsz, nq),
        in_specs=[pl.BlockSpec((IDX_HEADS * MXU_DIM, TQ), lambda b, i: (0, b * nq + i)),
                  pl.BlockSpec((None, seq, MXU_DIM), lambda b, i: (b, 0, 0)),
                  pl.BlockSpec((D_MODEL, TQ), lambda b, i: (0, b * nq + i)),
                  pl.BlockSpec((None, seq, D_MODEL), lambda b, i: (b, 0, 0)),
                  pl.BlockSpec((D_MODEL, seq), lambda b, i: (0, b)),
                  pl.BlockSpec((None, IDX_HEADS, TQ), lambda b, i: (b, 0, i))],
        out_specs=pl.BlockSpec((None, TQ, D_MODEL), lambda b, i: (b, i, 0)),
        out_shape=jax.ShapeDtypeStruct((bsz, seq, D_MODEL), _BF16),
        scratch_shapes=[pltpu.VMEM((nq, TQ, TQ), jnp.int32),
                        pltpu.VMEM((nq, TQ, TQ), jnp.int16),
                        pltpu.VMEM((nq, TQ, TQ), jnp.int16),
                        pltpu.VMEM((nq, TQ, TQ), _F32),
                        pltpu.VMEM((HEAD_GROUP, nq, TQ, TQ), _F32),
                        pltpu.VMEM((HEAD_GROUP, seq, TQ), _BF16),
                        pltpu.VMEM((8, TQ), jnp.int32)],
        compiler_params=_cparams(("parallel", "arbitrary"), 58),
        name="dsa",
    )(qcat_t, kcat, q_t, k_r, vt, wt)


def _dnprep_kernel(x_ref, w_ref, o_ref):
    j = pl.program_id(1)
    seq, cw = x_ref.shape
    rt = 256
    halo = 16
    ri = lax.broadcasted_iota(jnp.int32, (rt, rt), 0)
    ci = lax.broadcasted_iota(jnp.int32, (rt, rt), 1)
    hr = lax.broadcasted_iota(jnp.int32, (halo, halo), 0)
    hc = lax.broadcasted_iota(jnp.int32, (halo, halo), 1)
    shifts = [jnp.where(ri - ci == d, 1.0, 0.0).astype(_BF16) for d in range(1, CONV_WIDTH)]
    carries = [jnp.where(hc - hr == halo - d, 1.0, 0.0).astype(_BF16) for d in range(1, CONV_WIDTH)]
    per_tensor = D_MODEL // cw
    qk_scale = jnp.where(j < per_tensor, HEAD_DIM ** -0.5, 1.0)

    def chunk(r, normalise):
        xb = x_ref[r * rt:(r + 1) * rt, :]
        acc = w_ref[CONV_WIDTH - 1:CONV_WIDTH, :] * xb.astype(_F32)
        for d in range(1, CONV_WIDTH):
            acc = acc + w_ref[CONV_WIDTH - 1 - d:CONV_WIDTH - d, :] * _dot(shifts[d - 1], xb)
        if r > 0:
            prev = x_ref[r * rt - halo:r * rt, :]
            head = acc[:halo]
            for d in range(1, CONV_WIDTH):
                head = head + w_ref[CONV_WIDTH - 1 - d:CONV_WIDTH - d, :] * _dot(carries[d - 1], prev)
            acc = jnp.concatenate([head, acc[halo:]], axis=0)
        y = _silu(acc)
        if not normalise:
            o_ref[r * rt:(r + 1) * rt, :] = y.astype(_BF16)
            return
        for h in range(cw // HEAD_DIM):
            sl = slice(h * HEAD_DIM, (h + 1) * HEAD_DIM)
            yh = y[:, sl]
            nrm = lax.rsqrt(jnp.sum(yh * yh, axis=-1, keepdims=True) + EPS) * qk_scale
            o_ref[r * rt:(r + 1) * rt, sl] = (yh * nrm).astype(_BF16)

    @pl.when(j < 2 * per_tensor)
    def _():
        for r in range(seq // rt):
            chunk(r, True)

    @pl.when(j >= 2 * per_tensor)
    def _():
        for r in range(seq // rt):
            chunk(r, False)


def _dnprep(big3, conv_w):
    bsz, seq, _ = big3.shape
    cw = 512
    nblk = 3 * D_MODEL // cw
    cb = CB_BQ * D_MODEL // cw
    return pl.pallas_call(
        _dnprep_kernel,
        grid=(bsz, nblk),
        in_specs=[pl.BlockSpec((None, seq, cw), lambda b, j: (b, 0, cb + j)),
                  pl.BlockSpec((CONV_WIDTH, cw), lambda b, j: (0, j))],
        out_specs=pl.BlockSpec((None, seq, cw), lambda b, j: (b, 0, j)),
        out_shape=jax.ShapeDtypeStruct((bsz, seq, 3 * D_MODEL), _BF16),
        compiler_params=_cparams(("parallel", "arbitrary"), 48),
        name="dnprep",
    )(big3, conv_w)


N_LEVELS = 5
M_INCL, M_STRICT, M_PAIR, M_EYE, M_OFF0 = 0, 1, 2, 3, 4


def _delta_kernel(q_ref, k_ref, v_ref, z_ref, scal_ref, scalt_ref, gain_ref, o_ref,
                  state_ref, vnew_ref, mask_ref, lbf_ref, tbf_ref, pbf_ref, rhs_ref,
                  intra_ref, u_ref, wq_ref, kdt_ref):
    g_idx = pl.program_id(1)
    n_chunks = GROUP // CHUNK
    n_sub = GROUP // SUB
    per_sub = SUB // CHUNK
    units = [(h, b) for h in range(N_HEADS) for b in range(n_sub)]

    @pl.when(g_idx == 0)
    def _():
        state_ref[...] = jnp.zeros_like(state_ref)
        vnew_ref[...] = jnp.zeros_like(vnew_ref)
        ri = lax.broadcasted_iota(jnp.int32, (SUB, SUB), 0)
        ci = lax.broadcasted_iota(jnp.int32, (SUB, SUB), 1)
        same = (ri // CHUNK) == (ci // CHUNK)
        mask_ref[M_INCL] = jnp.where(same & (ri >= ci), 0.0, -jnp.inf)
        mask_ref[M_STRICT] = jnp.where(same & (ri > ci), 1.0, 0.0)
        mask_ref[M_PAIR] = jnp.where((ri // 2) == (ci // 2), 1.0, 0.0)
        mask_ref[M_EYE] = jnp.where(ri == ci, 1.0, 0.0)
        for lv in range(N_LEVELS):
            blk = 4 << lv
            off = ((ri // blk) == (ci // blk)) & ((ri // (blk // 2)) != (ci // (blk // 2)))
            mask_ref[M_OFF0 + lv] = jnp.where(off, 1.0, 0.0)

    for h, b in units:
        sl = slice(h * HEAD_DIM, (h + 1) * HEAD_DIM)
        rb = slice(b * SUB, (b + 1) * SUB)
        q = q_ref[rb, sl].astype(_F32)
        k_bf = k_ref[rb, sl]
        k = k_bf.astype(_F32)
        v = v_ref[rb, sl].astype(_F32)
        beta_c = scal_ref[rb, 8 + h:9 + h]
        gc_c = scal_ref[rb, 16 + h:17 + h]
        gl_c = scal_ref[rb, 24 + h:25 + h]
        gc_r = scalt_ref[16 + h:17 + h, rb]
        decay = jnp.exp((gc_c - gc_r) + mask_ref[M_INCL])
        kb = k * beta_c
        lmat = _dot_nt(kb.astype(_BF16), k_bf) * decay * mask_ref[M_STRICT]
        lbf_ref[h, b] = lmat.astype(_BF16)
        tbf_ref[h, b] = (mask_ref[M_EYE] - lmat * mask_ref[M_PAIR]).astype(_BF16)
        intra_ref[h, b] = (_dot_nt(q.astype(_BF16), k_bf) * decay).astype(_BF16)
        egc = jnp.exp(gc_c)
        rhs_ref[h, rb, 0:HEAD_DIM] = (v * beta_c).astype(_BF16)
        rhs_ref[h, rb, HEAD_DIM:2 * HEAD_DIM] = (kb * egc).astype(_BF16)
        qg = (q * egc).astype(_BF16)
        for c in range(per_sub):
            wq_ref[h, b * per_sub + c, CHUNK:2 * CHUNK, :] = qg[c * CHUNK:(c + 1) * CHUNK]
        kdt_ref[h, b] = (k * jnp.exp(gl_c - gc_c)).T.astype(_BF16)

    for lv in range(N_LEVELS):
        for h, b in units:
            p = _dot(lbf_ref[h, b], tbf_ref[h, b]) * mask_ref[M_OFF0 + lv]
            pbf_ref[h, b] = p.astype(_BF16)
        for h, b in units:
            t = tbf_ref[h, b]
            tbf_ref[h, b] = t - _dot(t, pbf_ref[h, b]).astype(_BF16)

    for h, b in units:
        rb = slice(b * SUB, (b + 1) * SUB)
        uw = _dot(tbf_ref[h, b], rhs_ref[h, rb, :])
        u_ref[h, rb, :] = uw[:, :HEAD_DIM]
        w_bf = uw[:, HEAD_DIM:].astype(_BF16)
        for c in range(per_sub):
            wq_ref[h, b * per_sub + c, 0:CHUNK, :] = w_bf[c * CHUNK:(c + 1) * CHUNK]

    lane_chunk = lax.broadcasted_iota(jnp.int32, (1, SUB), 1) // CHUNK
    for n in range(n_chunks):
        b, c = divmod(n, per_sub)
        rs = slice(n * CHUNK, (n + 1) * CHUNK)
        rc = slice(c * CHUNK, (c + 1) * CHUNK)
        for h in range(N_HEADS):
            ws = _dot(wq_ref[h, n], state_ref[h].astype(_BF16))
            vnew_ref[h, b, rc, :] = (u_ref[h, rs, :] - ws[:CHUNK]).astype(_BF16)
            u_ref[h, rs, :] = ws[CHUNK:]
        for h in range(N_HEADS):
            sl = slice(h * HEAD_DIM, (h + 1) * HEAD_DIM)
            vn = vnew_ref[h, b]
            o_n = u_ref[h, rs, :] + _dot(intra_ref[h, b, rc, :], vn)
            kd = jnp.where(lane_chunk == c, kdt_ref[h, b], jnp.zeros((), _BF16))
            gl = scal_ref[n * CHUNK:n * CHUNK + 1, 24 + h:25 + h]
            state_ref[h] = state_ref[h] * jnp.exp(gl) + _dot(kd, vn)
            ms = jnp.mean(o_n * o_n, axis=-1, keepdims=True)
            zn = z_ref[rs, sl].astype(_F32)
            o_ref[rs, sl] = (o_n * lax.rsqrt(ms + EPS) * gain_ref[...] * _silu(zn)).astype(_BF16)


def _delta(qkv, big3, scal3, scalt, gain):
    bsz, seq, _ = qkv.shape
    ng = seq // GROUP
    n_sub = GROUP // SUB
    hm = (N_HEADS, n_sub, SUB, SUB)
    return pl.pallas_call(
        _delta_kernel,
        grid=(bsz, ng),
        in_specs=[pl.BlockSpec((None, GROUP, D_MODEL), lambda b, g: (b, g, 0)),
                  pl.BlockSpec((None, GROUP, D_MODEL), lambda b, g: (b, g, 1)),
                  pl.BlockSpec((None, GROUP, D_MODEL), lambda b, g: (b, g, 2)),
                  pl.BlockSpec((None, GROUP, D_MODEL), lambda b, g: (b, g, CB_BZ)),
                  pl.BlockSpec((None, GROUP, LANES), lambda b, g: (b, g, 0)),
                  pl.BlockSpec((None, 32, GROUP), lambda b, g: (b, 0, g)),
                  pl.BlockSpec((1, HEAD_DIM), lambda b, g: (0, 0))],
        out_specs=pl.BlockSpec((None, GROUP, D_MODEL), lambda b, g: (b, g, 0)),
        out_shape=jax.ShapeDtypeStruct((bsz, seq, D_MODEL), _BF16),
        scratch_shapes=[pltpu.VMEM((N_HEADS, HEAD_DIM, HEAD_DIM), _F32),
                        pltpu.VMEM((N_HEADS, n_sub, SUB, HEAD_DIM), _BF16),
                        pltpu.VMEM((M_OFF0 + N_LEVELS, SUB, SUB), _F32),
                        pltpu.VMEM(hm, _BF16),
                        pltpu.VMEM(hm, _BF16),
                        pltpu.VMEM(hm, _BF16),
                        pltpu.VMEM((N_HEADS, GROUP, 2 * HEAD_DIM), _BF16),
                        pltpu.VMEM(hm, _BF16),
                        pltpu.VMEM((N_HEADS, GROUP, HEAD_DIM), _F32),
                        pltpu.VMEM((N_HEADS, GROUP // CHUNK, 2 * CHUNK, HEAD_DIM), _BF16),
                        pltpu.VMEM((N_HEADS, n_sub, HEAD_DIM, SUB), _BF16)],
        compiler_params=_cparams(("parallel", "arbitrary"), 48),
        name="delta",
    )(qkv, qkv, qkv, big3, scal3, scalt, gain)


def _out_kernel(oa_ref, az_ref, ob_ref, ga_ref, gb_ref, x_ref, mod_ref, gain_ref,
                wa_ref, wb_ref, wo_ref, o_ref):
    za = (oa_ref[...].astype(_F32) * _silu(az_ref[...].astype(_F32))).astype(_BF16)
    ya = _dot(za, wa_ref[...])
    yb = _dot(ob_ref[...], wb_ref[...])
    y = jax.nn.sigmoid(ga_ref[...].astype(_F32)) * ya + jax.nn.sigmoid(gb_ref[...].astype(_F32)) * yb
    out = _dot(y.astype(_BF16), wo_ref[...])
    ms = jnp.mean(out * out, axis=-1, keepdims=True)
    gate = mod_ref[0, :, 2 * D_MODEL:3 * D_MODEL]
    o_ref[...] = x_ref[...] + gate * (out * lax.rsqrt(ms + EPS) * gain_ref[...])


def _out(oa2, big, ob2, x2, mod3, post_gain, wa, wb, wo, seq):
    m = x2.shape[0]
    tm = 512
    per_b = seq // tm
    row = lambda i: (i, 0)
    const = lambda i: (0, 0)
    return pl.pallas_call(
        _out_kernel,
        grid=(m // tm,),
        in_specs=[pl.BlockSpec((tm, D_MODEL), row),
                  pl.BlockSpec((tm, D_MODEL), lambda i: (i, CB_AZ)),
                  pl.BlockSpec((tm, D_MODEL), row),
                  pl.BlockSpec((tm, D_MODEL), lambda i: (i, CB_GA)),
                  pl.BlockSpec((tm, D_MODEL), lambda i: (i, CB_GB)),
                  pl.BlockSpec((tm, D_MODEL), row),
                  pl.BlockSpec((1, 1, 3 * D_MODEL), lambda i: (i // per_b, 0, 0)),
                  pl.BlockSpec((1, D_MODEL), const),
                  pl.BlockSpec((D_MODEL, D_MODEL), const),
                  pl.BlockSpec((D_MODEL, D_MODEL), const),
                  pl.BlockSpec((D_MODEL, D_MODEL), const)],
        out_specs=pl.BlockSpec((tm, D_MODEL), row),
        out_shape=jax.ShapeDtypeStruct((m, D_MODEL), _F32),
        compiler_params=_cparams(("parallel",), 48),
        name="out",
    )(oa2, big, ob2, big, big, x2, mod3, post_gain, wa, wb, wo)


def _pad_lanes(v, start):
    return jnp.zeros((1, LANES), _F32).at[0, start:start + v.shape[0]].set(v)


def _layer(x, c, positions, w_ada, b_ada, pre_gain, post_gain, w_in, ln_gain, ln_bias,
           conv_w, a_log, dt_bias, dn_gain, w_a_out, w_b_out, w_o):
    bsz, seq, d = x.shape
    m = bsz * seq
    topk = min(TOPK_MAX, seq // 4)

    pts = np.cumsum([0, 1024, 1024, 1024, 1024, 512, 64, 8, 1024, 1024, 1024, 1024, 8, 8, 1024, 1024])
    col = lambda i: w_in[:, pts[i]:pts[i + 1]]
    (aq, ak, av, az, iq, ik, iw, bq, bk, bv, bz, bbeta, ba, ga, gb) = [col(i) for i in range(15)]
    w_big = jnp.concatenate([ak, az, bq, bk, bv, bz, ga, gb], axis=1).astype(_BF16)
    w_qvt = jnp.concatenate([aq, av], axis=1).T.astype(_BF16)
    w_small = jnp.concatenate(
        [ik, ik, iw, bbeta, ba, jnp.zeros((d, SMALL_W - 152), _F32)], axis=1)
    ws_hi = w_small.astype(_BF16)
    ws_lo = (w_small - ws_hi.astype(_F32)).astype(_BF16)
    iq_t = iq.T
    wiq_hi = iq_t.astype(_BF16)
    wiq_lo = (iq_t - wiq_hi.astype(_F32)).astype(_BF16)

    mod = _ada(c, w_ada, b_ada)
    mod3 = mod.reshape(bsz, 1, 3 * d)
    x2 = x.reshape(m, d)
    big, small, qt, vt, iqt = _proj(x2, mod3, pre_gain.reshape(1, d), w_big, ws_hi, ws_lo,
                                    w_qvt, wiq_hi, wiq_lo, seq)

    rot_a = HEAD_DIM // ROPE_FRACTION
    rot_i = IDX_DIM // ROPE_FRACTION
    invf_a = ROPE_THETA ** (-(jnp.arange(rot_a // 2, dtype=_F32) * 2.0 / rot_a))
    invf_i = ROPE_THETA ** (-(jnp.arange(rot_i // 2, dtype=_F32) * 2.0 / rot_i))
    lng = jnp.concatenate([ln_gain, ln_gain]).reshape(1, LANES)
    lnb = jnp.concatenate([ln_bias, ln_bias]).reshape(1, LANES)
    q_t, k_r, qcat_t, kcat, scal = _prep(big, qt, iqt, small, positions.reshape(1, m),
                                         invf_a.reshape(-1, 1), invf_i.reshape(-1, 1), lng, lnb,
                                         _pad_lanes(a_log, 16), _pad_lanes(dt_bias, 16))

    scal3 = scal.reshape(bsz, seq, LANES)
    scalt = jnp.transpose(scal3[:, :, :32], (0, 2, 1))
    o_a = _dsa(qcat_t, kcat.reshape(bsz, seq, -1), q_t, k_r.reshape(bsz, seq, d),
               vt, scalt[:, 0:IDX_HEADS, :], topk)

    big3 = big.reshape(bsz, seq, N_BIG)
    qkv = _dnprep(big3, conv_w)
    o_b = _delta(qkv, big3, scal3, scalt, dn_gain.reshape(1, HEAD_DIM))

    y = _out(o_a.reshape(m, d), big, o_b.reshape(m, d), x2, mod3, post_gain.reshape(1, d),
             w_a_out.astype(_BF16), w_b_out.astype(_BF16), w_o.astype(_BF16), seq)
    return y.reshape(bsz, seq, d)


def kernel(x, c, positions, w_ada, b_ada, pre_norm_gain, post_norm_gain, w_in, idx_k_ln_gain,
           idx_k_ln_bias, dn_conv_w, dn_a_log, dn_dt_bias, dn_norm_gain, w_a_out, w_b_out, w_o):
    for layer in range(w_ada.shape[0]):
        x = _layer(x, c, positions, w_ada[layer], b_ada[layer], pre_norm_gain[layer],
                   post_norm_gain[layer], w_in[layer], idx_k_ln_gain[layer], idx_k_ln_bias[layer],
                   dn_conv_w[layer], dn_a_log[layer], dn_dt_bias[layer], dn_norm_gain[layer],
                   w_a_out[layer], w_b_out[layer], w_o[layer])
    return x
```

```python
import functools

import numpy as np
import jax
import jax.numpy as jnp
from jax import lax
from jax.experimental import pallas as pl
from jax.experimental.pallas import tpu as pltpu

D_MODEL = 1024
N_HEADS = 8
HEAD_DIM = 128
IDX_HEADS = 8
IDX_DIM = 64
TOPK_MAX = 256
CONV_WIDTH = 4
CHUNK = 64
ROPE_THETA = 500000.0
ROPE_FRACTION = 4
EPS = 1e-6

LANES = 128
MXU_DIM = 256
TQ = MXU_DIM
GROUP = MXU_DIM
SUB = 128
SMALL_W = 256
IQ_W = IDX_HEADS * IDX_DIM
NEG_BIG = -1e30
HEAD_GROUP = 8
LOG2E = 1.4426950408889634
COUNT_LANES = 8
INT_MIN = -2147483648

CB_AK, CB_AZ, CB_BQ, CB_BK, CB_BV, CB_BZ, CB_GA, CB_GB = range(8)
N_BIG = 8 * D_MODEL

_F32 = jnp.float32
_BF16 = jnp.bfloat16


def _cparams(sem, vmem_mb):
    return pltpu.CompilerParams(dimension_semantics=sem, vmem_limit_bytes=vmem_mb * 1024 * 1024)


def _split2(a):
    hi = a.astype(_BF16)
    lo = (a - hi.astype(_F32)).astype(_BF16)
    return hi, lo


def _dot(a, b):
    return jnp.dot(a, b, preferred_element_type=_F32)


def _dot_nt(a, b):
    return lax.dot_general(a, b, (((1,), (1,)), ((), ())), preferred_element_type=_F32)


def _silu(x):
    return x * jax.nn.sigmoid(x)


def _ada_kernel(c_ref, w_ref, b_ref, o_ref):
    c1, c2 = _split2(c_ref[...])
    c3 = (c_ref[...] - c1.astype(_F32) - c2.astype(_F32)).astype(_BF16)
    w = w_ref[...]
    w1, w2 = _split2(w)
    w3 = (w - w1.astype(_F32) - w2.astype(_F32)).astype(_BF16)
    acc = _dot(c1, w3) + _dot(c2, w2) + _dot(c3, w1)
    acc = acc + _dot(c1, w2) + _dot(c2, w1)
    acc = acc + _dot(c1, w1)
    o_ref[...] = acc + b_ref[...]


def _ada(c, w_ada, b_ada):
    bsz = c.shape[0]
    n = w_ada.shape[1]
    tn = 512
    return pl.pallas_call(
        _ada_kernel,
        grid=(n // tn,),
        in_specs=[pl.BlockSpec((bsz, D_MODEL), lambda j: (0, 0)),
                  pl.BlockSpec((D_MODEL, tn), lambda j: (0, j)),
                  pl.BlockSpec((1, tn), lambda j: (0, j))],
        out_specs=pl.BlockSpec((bsz, tn), lambda j: (0, j)),
        out_shape=jax.ShapeDtypeStruct((bsz, n), _F32),
        compiler_params=_cparams(("arbitrary",), 32),
        name="ada",
    )(c, w_ada, b_ada.reshape(1, n))


def _proj_kernel(x_ref, mod_ref, gain_ref, wbig_ref, wsh_ref, wsl_ref, wqvt_ref, wiqh_ref, wiql_ref,
                 big_ref, small_ref, qt_ref, vt_ref, iqt_ref, h_ref):
    j = pl.program_id(1)

    @pl.when(j == 0)
    def _():
        x = x_ref[...]
        ms = jnp.mean(x * x, axis=-1, keepdims=True)
        y = x * lax.rsqrt(ms + EPS) * gain_ref[...]
        shift = mod_ref[0, :, 0:D_MODEL]
        scale = mod_ref[0, :, D_MODEL:2 * D_MODEL]
        h = y * (1.0 + scale) + shift
        hi, lo = _split2(h)
        h_ref[...] = hi
        small_ref[...] = (_dot(hi, wsl_ref[...]) + _dot(lo, wsh_ref[...])) + _dot(hi, wsh_ref[...])
        qv = _dot_nt(wqvt_ref[...], hi).astype(_BF16)
        qt_ref[...] = qv[:D_MODEL]
        vt_ref[...] = qv[D_MODEL:]
        iqt_ref[...] = (_dot_nt(wiql_ref[...], hi) + _dot_nt(wiqh_ref[...], lo)) + _dot_nt(wiqh_ref[...], hi)

    big_ref[...] = _dot(h_ref[...], wbig_ref[...]).astype(_BF16)


def _proj(x2, mod3, pre_gain, w_big, ws_hi, ws_lo, w_qvt, wiq_hi, wiq_lo, seq):
    m = x2.shape[0]
    tm, tn = min(1024, seq), 1024
    per_b = seq // tm
    const = lambda i, j: (0, 0)
    colblk = lambda i, j: (0, i)
    once = pl.Buffered(1)
    return pl.pallas_call(
        _proj_kernel,
        grid=(m // tm, N_BIG // tn),
        in_specs=[pl.BlockSpec((tm, D_MODEL), lambda i, j: (i, 0)),
                  pl.BlockSpec((1, 1, 3 * D_MODEL), lambda i, j: (i // per_b, 0, 0)),
                  pl.BlockSpec((1, D_MODEL), const),
                  pl.BlockSpec((D_MODEL, tn), lambda i, j: (0, j)),
                  pl.BlockSpec((D_MODEL, SMALL_W), const, pipeline_mode=once),
                  pl.BlockSpec((D_MODEL, SMALL_W), const, pipeline_mode=once),
                  pl.BlockSpec((2 * D_MODEL, D_MODEL), const, pipeline_mode=once),
                  pl.BlockSpec((IQ_W, D_MODEL), const, pipeline_mode=once),
                  pl.BlockSpec((IQ_W, D_MODEL), const, pipeline_mode=once)],
        out_specs=[pl.BlockSpec((tm, tn), lambda i, j: (i, j)),
                   pl.BlockSpec((tm, SMALL_W), lambda i, j: (i, 0)),
                   pl.BlockSpec((D_MODEL, tm), colblk),
                   pl.BlockSpec((D_MODEL, tm), colblk),
                   pl.BlockSpec((IQ_W, tm), colblk)],
        out_shape=[jax.ShapeDtypeStruct((m, N_BIG), _BF16),
                   jax.ShapeDtypeStruct((m, SMALL_W), _F32),
                   jax.ShapeDtypeStruct((D_MODEL, m), _BF16),
                   jax.ShapeDtypeStruct((D_MODEL, m), _BF16),
                   jax.ShapeDtypeStruct((IQ_W, m), _F32)],
        scratch_shapes=[pltpu.VMEM((tm, D_MODEL), _BF16)],
        compiler_params=_cparams(("parallel", "arbitrary"), 58),
        name="proj",
    )(x2, mod3, pre_gain, w_big, ws_hi, ws_lo, w_qvt, wiq_hi, wiq_lo)


def _rope(x, c, sn, sp, half):
    return x * c + pltpu.roll(x, LANES - half, 1) * sn + pltpu.roll(x, half, 1) * sp


def _prep_kernel(ak_ref, qt_ref, iqt_ref, small_ref, pos_ref, fa_ref, fi_ref, lng_ref, lnb_ref,
                 alog_ref, dtb_ref, qto_ref, k_ref, qcat_ref, kcat_ref, scal_ref):
    tr = ak_ref.shape[0]
    ha, hi_ = 16, 8
    posf = pos_ref[...].astype(_F32)
    ang_a = fa_ref[...] * posf
    cos_a, sin_a = jnp.cos(ang_a), jnp.sin(ang_a)
    ang_i = fi_ref[...] * posf
    cos_i, sin_i = jnp.cos(ang_i), jnp.sin(ang_i)

    q_scale = HEAD_DIM ** -0.5 * LOG2E
    for h in range(N_HEADS):
        b = h * HEAD_DIM
        x1 = qt_ref[b:b + ha, :].astype(_F32)
        x2 = qt_ref[b + ha:b + 2 * ha, :].astype(_F32)
        qto_ref[b:b + ha, :] = ((x1 * cos_a - x2 * sin_a) * q_scale).astype(_BF16)
        qto_ref[b + ha:b + 2 * ha, :] = ((x2 * cos_a + x1 * sin_a) * q_scale).astype(_BF16)
        rest = qt_ref[b + 2 * ha:b + HEAD_DIM, :].astype(_F32)
        qto_ref[b + 2 * ha:b + HEAD_DIM, :] = (rest * q_scale).astype(_BF16)

    for h in range(IDX_HEADS):
        b = h * IDX_DIM
        x1 = iqt_ref[b:b + hi_, :]
        x2 = iqt_ref[b + hi_:b + 2 * hi_, :]
        y = jnp.concatenate([x1 * cos_i - x2 * sin_i, x2 * cos_i + x1 * sin_i,
                             iqt_ref[b + 2 * hi_:b + IDX_DIM, :]], axis=0) * (IDX_DIM ** -0.5)
        yh = y.astype(_BF16)
        yl = (y - yh.astype(_F32)).astype(_BF16)
        o = h * MXU_DIM
        qcat_ref[o:o + IDX_DIM, :] = yh
        qcat_ref[o + IDX_DIM:o + 2 * IDX_DIM, :] = yh
        qcat_ref[o + 2 * IDX_DIM:o + 3 * IDX_DIM, :] = yl
        qcat_ref[o + 3 * IDX_DIM:o + 4 * IDX_DIM, :] = yl

    one_a = jnp.ones((LANES - 2 * ha, tr), _F32)
    zero = lambda n: jnp.zeros((n, tr), _F32)
    c_a = jnp.concatenate([cos_a, cos_a, one_a], axis=0).T
    sn_a = jnp.concatenate([-sin_a, zero(LANES - ha)], axis=0).T
    sp_a = jnp.concatenate([zero(ha), sin_a, zero(LANES - 2 * ha)], axis=0).T
    one_i = jnp.ones((IDX_DIM - 2 * hi_, tr), _F32)
    c_i = jnp.concatenate([cos_i, cos_i, one_i] * 2, axis=0).T
    sn_i = jnp.concatenate([-sin_i, zero(IDX_DIM - hi_)] * 2, axis=0).T
    sp_i = jnp.concatenate([zero(hi_), sin_i, zero(IDX_DIM - 2 * hi_)] * 2, axis=0).T

    for h in range(N_HEADS):
        sl = slice(h * HEAD_DIM, (h + 1) * HEAD_DIM)
        xk = ak_ref[:, sl].astype(_F32)
        k_ref[:, sl] = _rope(xk, c_a, sn_a, sp_a, ha).astype(_BF16)

    lane = lax.broadcasted_iota(jnp.int32, (1, LANES), 1)
    first = lane < IDX_DIM
    kk = small_ref[:, 0:LANES]
    mu = jnp.mean(kk, axis=-1, keepdims=True)
    var = jnp.mean(jnp.square(kk - mu), axis=-1, keepdims=True)
    kn = (kk - mu) * lax.rsqrt(var + EPS) * lng_ref[...] + lnb_ref[...]
    kr = _rope(kn, c_i, sn_i, sp_i, hi_)
    khi = kr.astype(_BF16).astype(_F32)
    kpair = jnp.where(first, khi, kr - khi).astype(_BF16)
    kcat_ref[:, 0:LANES] = kpair
    kcat_ref[:, LANES:2 * LANES] = kpair

    g = small_ref[:, LANES:2 * LANES]
    w_s = g * (IDX_HEADS ** -0.5)
    beta = jax.nn.sigmoid(g)
    z = g + dtb_ref[...]
    softplus = jnp.maximum(z, 0.0) + jnp.log1p(jnp.exp(-jnp.abs(z)))
    gg = -jnp.exp(alog_ref[...]) * softplus
    rowc = lax.broadcasted_iota(jnp.int32, (tr, LANES), 0) % CHUNK
    fwd = gg
    rev = gg
    s = 1
    while s < CHUNK:
        fwd = fwd + jnp.where(rowc >= s, pltpu.roll(fwd, s, 0), 0.0)
        rev = rev + jnp.where(rowc < CHUNK - s, pltpu.roll(rev, tr - s, 0), 0.0)
        s *= 2
    glast = fwd + rev - gg
    out = jnp.where(lane < 8, w_s, jnp.where(lane < 16, beta, jnp.where(lane < 24, fwd, 0.0)))
    out = jnp.where((lane >= 24) & (lane < 32), pltpu.roll(glast, 8, 1), out)
    scal_ref[...] = out


def _prep(big, qt, iqt, small, pos_row, fa, fi, lng, lnb, alog, dtb):
    m = big.shape[0]
    tr = 512
    row = lambda i: (i, 0)
    col = lambda i: (0, i)
    const = lambda i: (0, 0)
    return pl.pallas_call(
        _prep_kernel,
        grid=(m // tr,),
        in_specs=[pl.BlockSpec((tr, D_MODEL), lambda i: (i, CB_AK)),
                  pl.BlockSpec((D_MODEL, tr), col),
                  pl.BlockSpec((IQ_W, tr), col),
                  pl.BlockSpec((tr, SMALL_W), row),
                  pl.BlockSpec((1, tr), col),
                  pl.BlockSpec((16, 1), const), pl.BlockSpec((8, 1), const),
                  pl.BlockSpec((1, LANES), const), pl.BlockSpec((1, LANES), const),
                  pl.BlockSpec((1, LANES), const), pl.BlockSpec((1, LANES), const)],
        out_specs=[pl.BlockSpec((D_MODEL, tr), col), pl.BlockSpec((tr, D_MODEL), row),
                   pl.BlockSpec((IDX_HEADS * MXU_DIM, tr), col), pl.BlockSpec((tr, MXU_DIM), row),
                   pl.BlockSpec((tr, LANES), row)],
        out_shape=[jax.ShapeDtypeStruct((D_MODEL, m), _BF16), jax.ShapeDtypeStruct((m, D_MODEL), _BF16),
                   jax.ShapeDtypeStruct((IDX_HEADS * MXU_DIM, m), _BF16),
                   jax.ShapeDtypeStruct((m, MXU_DIM), _BF16),
                   jax.ShapeDtypeStruct((m, LANES), _F32)],
        compiler_params=_cparams(("parallel",), 48),
        name="prep",
    )(big, qt, iqt, small, pos_row, fa, fi, lng, lnb, alog, dtb)


def _tile_loop(nk, body, init):
    def pair(i, c):
        return body(2 * i + 1, body(2 * i, c))
    c = lax.fori_loop(0, nk // 2, pair, init)
    return lax.cond(nk % 2 == 1, lambda c: body(nk - 1, c), lambda c: c, c)


def _dsa_kernel(topk, qcat_ref, kcat_ref, q_ref, k_ref, vt_ref, wt_ref, o_ref,
                keys_ref, hi16_ref, lo16_ref, bias_ref, s_ref, p_ref, thr_ref):
    qi = pl.program_id(1)
    nk = qi + 1
    kt_rows = lax.broadcasted_iota(jnp.int32, (TQ, TQ), 0)
    row_minus_col = kt_rows - lax.broadcasted_iota(jnp.int32, (TQ, TQ), 1)

    def score_tile(kt, carry):
        r0 = pl.multiple_of(kt * TQ, TQ)
        kc = kcat_ref[pl.ds(r0, TQ), :]
        acc = jnp.zeros((TQ, TQ), _F32)
        for h in range(IDX_HEADS):
            lg = _dot(kc, qcat_ref[h * MXU_DIM:(h + 1) * MXU_DIM, :])
            acc = acc + wt_ref[h:h + 1, :] * jnp.maximum(lg, 0.0)
        bits = pltpu.bitcast(acc + 0.0, jnp.int32)
        key = bits ^ ((bits >> 31) & 0x7FFFFFFF)
        key = jnp.where(row_minus_col <= (qi - kt) * TQ, key, INT_MIN)
        keys_ref[kt] = key
        hi16_ref[kt] = (key >> 16).astype(jnp.int16)
        lo16_ref[kt] = (key - 32768).astype(jnp.int16)
        return carry

    _tile_loop(nk, score_tile, 0)

    def search_static(n_tiles):
        def count16(ref, pred_fn):
            accs = [jnp.zeros((16, TQ), jnp.int16) for _ in range(COUNT_LANES)]
            for kt in range(n_tiles):
                m = jnp.where(pred_fn(ref[kt]), jnp.int16(1), jnp.int16(0))
                for r in range(TQ // 16):
                    accs[r % COUNT_LANES] = accs[r % COUNT_LANES] + m[r * 16:(r + 1) * 16]
            acc = functools.reduce(lambda a, b: a + b, accs)
            return jnp.sum(acc.astype(jnp.int32), axis=0, keepdims=True)

        def search16(ref, want, floor_bits=None):
            if floor_bits is None:
                cnt0 = count16(ref, lambda kv: kv >= jnp.int16(0))
                ok0 = cnt0 >= want
                thr = jnp.where(ok0, 0, -32768)
                cnt_thr = jnp.where(ok0, cnt0, n_tiles * TQ)
                n_bits = 15
            else:
                thr = jnp.full((1, TQ), -(1 << floor_bits), jnp.int32)
                cnt_thr = jnp.zeros((1, TQ), jnp.int32)
                n_bits = floor_bits

            def bit_step(it, carry):
                thr, cnt_thr = carry
                cand = thr + lax.shift_left(jnp.int32(1), n_bits - 1 - it)
                cand16 = cand.astype(jnp.int16)
                cnt = count16(ref, lambda kv: kv >= cand16)
                ok = cnt >= want
                return jnp.where(ok, cand, thr), jnp.where(ok, cnt, cnt_thr)

            return lax.fori_loop(0, n_bits, bit_step, (thr, cnt_thr))

        thr_hi, _ = search16(hi16_ref, topk)
        thr_hi16 = thr_hi.astype(jnp.int16)
        n_above = count16(hi16_ref, lambda kv: kv > thr_hi16)
        for kt in range(n_tiles):
            lo16_ref[kt] = jnp.where(hi16_ref[kt] == thr_hi16, lo16_ref[kt], jnp.int16(-32768))
        thr_lo, cnt_lo = search16(lo16_ref, topk - n_above)
        thr_lo16 = thr_lo.astype(jnp.int16)
        thr = thr_hi * 65536 + (thr_lo + 32768)
        tie_lane = jnp.where((n_above + cnt_lo > topk) & (thr > INT_MIN), 1, 0)
        thr_ref[0:1, :] = thr
        thr_ref[1:2, :] = tie_lane

        @pl.when(jnp.max(tie_lane) > 0)
        def _():
            need = topk - (n_above + count16(lo16_ref, lambda kv: kv > thr_lo16))
            for kt in range(n_tiles):
                neg_pos = (-1 - (kt_rows + kt * TQ)).astype(jnp.int16)
                at_thr = jnp.where(hi16_ref[kt] == thr_hi16, neg_pos, jnp.int16(-32768))
                lo16_ref[kt] = jnp.where(lo16_ref[kt] == thr_lo16, at_thr, jnp.int16(-32768))
            thr_pos, _ = search16(lo16_ref, need, floor_bits=pos_bits)
            thr_ref[2:3, :] = jnp.where(thr > INT_MIN, -1 - thr_pos, -1)

    pos_bits = max(1, (k_ref.shape[0] - 1).bit_length())
    assert pos_bits < 15, "positions are searched as negative int16 values"
    for n_tiles in range(1, keys_ref.shape[0] + 1):
        pl.when(nk == n_tiles)(functools.partial(search_static, n_tiles))
    thr = thr_ref[0:1, :]
    tie = jnp.max(thr_ref[1:2, :]) > 0

    @pl.when(jnp.logical_not(tie))
    def _():
        def body(kt, carry):
            kv = keys_ref[kt]
            bias_ref[kt] = jnp.where(kv >= thr_sel, 0.0, NEG_BIG)
            return carry
        thr_sel = jnp.maximum(thr, INT_MIN + 1)
        _tile_loop(nk, body, 0)

    @pl.when(tie)
    def _():
        last_pos = thr_ref[2:3, :]

        def body(kt, carry):
            kv = keys_ref[kt]
            at_thr = jnp.where((kt_rows + kt * TQ) <= last_pos, 0.0, NEG_BIG)
            bias_ref[kt] = jnp.where(kv > thr, 0.0, jnp.where(kv == thr, at_thr, NEG_BIG))
            return carry
        _tile_loop(nk, body, 0)

    def zero_tail(kt, carry):
        r0 = pl.multiple_of(kt * TQ, TQ)
        for i in range(HEAD_GROUP):
            p_ref[i, pl.ds(r0, TQ), :] = jnp.zeros((TQ, TQ), _BF16)
        return carry

    lax.fori_loop(nk, pl.num_programs(1), zero_tail, 0)
    for hg in range(N_HEADS // HEAD_GROUP):
        hs = [hg * HEAD_GROUP + i for i in range(HEAD_GROUP)]
        sls = [slice(h * HEAD_DIM, (h + 1) * HEAD_DIM) for h in hs]
        qhs = [q_ref[sl, :] for sl in sls]

        def s_tile(kt, mxs):
            r0 = pl.multiple_of(kt * TQ, TQ)
            bias = bias_ref[kt]
            out = []
            for i in range(HEAD_GROUP):
                s = _dot(k_ref[pl.ds(r0, TQ), sls[i]], qhs[i]) + bias
                s_ref[i, kt] = s
                out.append(jnp.maximum(mxs[i], jnp.max(s.reshape(TQ // 8, 8, TQ), axis=0)))
            return tuple(out)

        mxs = _tile_loop(nk, s_tile,
                            tuple(jnp.full((8, TQ), NEG_BIG, _F32) for _ in range(HEAD_GROUP)))
        mxs = [jnp.max(m, axis=0, keepdims=True) for m in mxs]

        def p_tile(kt, ls):
            r0 = pl.multiple_of(kt * TQ, TQ)
            out = []
            for i in range(HEAD_GROUP):
                p = jnp.exp2(s_ref[i, kt] - mxs[i])
                out.append(ls[i] + jnp.sum(p.reshape(TQ // 8, 8, TQ), axis=0))
                p_ref[i, pl.ds(r0, TQ), :] = p.astype(_BF16)
            return tuple(out)

        ls = lax.fori_loop(0, nk, p_tile,
                           tuple(jnp.zeros((8, TQ), _F32) for _ in range(HEAD_GROUP)))
        def pv(k_tiles):
            for i in range(HEAD_GROUP):
                l = jnp.sum(ls[i], axis=0, keepdims=True)
                acc = _dot(vt_ref[sls[i], 0:k_tiles * TQ], p_ref[i, 0:k_tiles * TQ, :])
                o_ref[:, sls[i]] = (acc / l).T.astype(_BF16)

        nq = keys_ref.shape[0]
        sizes = sorted({max(1, nq // 4), max(1, nq // 2), nq})
        for lo_t, k_tiles in zip([0] + sizes[:-1], sizes):
            pl.when((nk > lo_t) & (nk <= k_tiles))(functools.partial(pv, k_tiles))


def _dsa(qcat_t, kcat, q_t, k_r, vt, wt, topk):
    bsz, seq, _ = k_r.shape
    nq = seq // TQ
    return pl.pallas_call(
        functools.partial(_dsa_kernel, topk),
        grid=(bsz, nq),
        in_specs=[pl.BlockSpec((IDX_HEADS * MXU_DIM, TQ), lambda b, i: (0, b * nq + i)),
                  pl.BlockSpec((None, seq, MXU_DIM), lambda b, i: (b, 0, 0)),
                  pl.BlockSpec((D_MODEL, TQ), lambda b, i: (0, b * nq + i)),
                  pl.BlockSpec((None, seq, D_MODEL), lambda b, i: (b, 0, 0)),
                  pl.BlockSpec((D_MODEL, seq), lambda b, i: (0, b)),
                  pl.BlockSpec((None, IDX_HEADS, TQ), lambda b, i: (b, 0, i))],
        out_specs=pl.BlockSpec((None, TQ, D_MODEL), lambda b, i: (b, i, 0)),
        out_shape=jax.ShapeDtypeStruct((bsz, seq, D_MODEL), _BF16),
        scratch_shapes=[pltpu.VMEM((nq, TQ, TQ), jnp.int32),
                        pltpu.VMEM((nq, TQ, TQ), jnp.int16),
                        pltpu.VMEM((nq, TQ, TQ), jnp.int16),
                        pltpu.VMEM((nq, TQ, TQ), _F32),
                        pltpu.VMEM((HEAD_GROUP, nq, TQ, TQ), _F32),
                        pltpu.VMEM((HEAD_GROUP, seq, TQ), _BF16),
                        pltpu.VMEM((8, TQ), jnp.int32)],
        compiler_params=_cparams(("parallel", "arbitrary"), 58),
        name="dsa",
    )(qcat_t, kcat, q_t, k_r, vt, wt)


def _dnprep_kernel(x_ref, w_ref, o_ref):
    j = pl.program_id(1)
    seq, cw = x_ref.shape
    rt = 256
    halo = 16
    ri = lax.broadcasted_iota(jnp.int32, (rt, rt), 0)
    ci = lax.broadcasted_iota(jnp.int32, (rt, rt), 1)
    hr = lax.broadcasted_iota(jnp.int32, (halo, halo), 0)
    hc = lax.broadcasted_iota(jnp.int32, (halo, halo), 1)
    shifts = [jnp.where(ri - ci == d, 1.0, 0.0).astype(_BF16) for d in range(1, CONV_WIDTH)]
    carries = [jnp.where(hc - hr == halo - d, 1.0, 0.0).astype(_BF16) for d in range(1, CONV_WIDTH)]
    per_tensor = D_MODEL // cw
    qk_scale = jnp.where(j < per_tensor, HEAD_DIM ** -0.5, 1.0)

    def chunk(r, normalise):
        xb = x_ref[r * rt:(r + 1) * rt, :]
        acc = w_ref[CONV_WIDTH - 1:CONV_WIDTH, :] * xb.astype(_F32)
        for d in range(1, CONV_WIDTH):
            acc = acc + w_ref[CONV_WIDTH - 1 - d:CONV_WIDTH - d, :] * _dot(shifts[d - 1], xb)
        if r > 0:
            prev = x_ref[r * rt - halo:r * rt, :]
            head = acc[:halo]
            for d in range(1, CONV_WIDTH):
                head = head + w_ref[CONV_WIDTH - 1 - d:CONV_WIDTH - d, :] * _dot(carries[d - 1], prev)
            acc = jnp.concatenate([head, acc[halo:]], axis=0)
        y = _silu(acc)
        if not normalise:
            o_ref[r * rt:(r + 1) * rt, :] = y.astype(_BF16)
            return
        for h in range(cw // HEAD_DIM):
            sl = slice(h * HEAD_DIM, (h + 1) * HEAD_DIM)
            yh = y[:, sl]
            nrm = lax.rsqrt(jnp.sum(yh * yh, axis=-1, keepdims=True) + EPS) * qk_scale
            o_ref[r * rt:(r + 1) * rt, sl] = (yh * nrm).astype(_BF16)

    @pl.when(j < 2 * per_tensor)
    def _():
        for r in range(seq // rt):
            chunk(r, True)

    @pl.when(j >= 2 * per_tensor)
    def _():
        for r in range(seq // rt):
            chunk(r, False)


def _dnprep(big3, conv_w):
    bsz, seq, _ = big3.shape
    cw = 512
    nblk = 3 * D_MODEL // cw
    cb = CB_BQ * D_MODEL // cw
    return pl.pallas_call(
        _dnprep_kernel,
        grid=(bsz, nblk),
        in_specs=[pl.BlockSpec((None, seq, cw), lambda b, j: (b, 0, cb + j)),
                  pl.BlockSpec((CONV_WIDTH, cw), lambda b, j: (0, j))],
        out_specs=pl.BlockSpec((None, seq, cw), lambda b, j: (b, 0, j)),
        out_shape=jax.ShapeDtypeStruct((bsz, seq, 3 * D_MODEL), _BF16),
        compiler_params=_cparams(("parallel", "arbitrary"), 48),
        name="dnprep",
    )(big3, conv_w)


N_LEVELS = 5
M_INCL, M_STRICT, M_PAIR, M_EYE, M_OFF0 = 0, 1, 2, 3, 4


def _delta_kernel(q_ref, k_ref, v_ref, z_ref, scal_ref, scalt_ref, gain_ref, o_ref,
                  state_ref, vnew_ref, mask_ref, lbf_ref, tbf_ref, pbf_ref, rhs_ref,
                  intra_ref, u_ref, wq_ref, kdt_ref):
    g_idx = pl.program_id(1)
    n_chunks = GROUP // CHUNK
    n_sub = GROUP // SUB
    per_sub = SUB // CHUNK
    units = [(h, b) for h in range(N_HEADS) for b in range(n_sub)]

    @pl.when(g_idx == 0)
    def _():
        state_ref[...] = jnp.zeros_like(state_ref)
        vnew_ref[...] = jnp.zeros_like(vnew_ref)
        ri = lax.broadcasted_iota(jnp.int32, (SUB, SUB), 0)
        ci = lax.broadcasted_iota(jnp.int32, (SUB, SUB), 1)
        same = (ri // CHUNK) == (ci // CHUNK)
        mask_ref[M_INCL] = jnp.where(same & (ri >= ci), 0.0, -jnp.inf)
        mask_ref[M_STRICT] = jnp.where(same & (ri > ci), 1.0, 0.0)
        mask_ref[M_PAIR] = jnp.where((ri // 2) == (ci // 2), 1.0, 0.0)
        mask_ref[M_EYE] = jnp.where(ri == ci, 1.0, 0.0)
        for lv in range(N_LEVELS):
            blk = 4 << lv
            off = ((ri // blk) == (ci // blk)) & ((ri // (blk // 2)) != (ci // (blk // 2)))
            mask_ref[M_OFF0 + lv] = jnp.where(off, 1.0, 0.0)

    for h, b in units:
        sl = slice(h * HEAD_DIM, (h + 1) * HEAD_DIM)
        rb = slice(b * SUB, (b + 1) * SUB)
        q = q_ref[rb, sl].astype(_F32)
        k_bf = k_ref[rb, sl]
        k = k_bf.astype(_F32)
        v = v_ref[rb, sl].astype(_F32)
        beta_c = scal_ref[rb, 8 + h:9 + h]
        gc_c = scal_ref[rb, 16 + h:17 + h]
        gl_c = scal_ref[rb, 24 + h:25 + h]
        gc_r = scalt_ref[16 + h:17 + h, rb]
        decay = jnp.exp((gc_c - gc_r) + mask_ref[M_INCL])
        kb = k * beta_c
        lmat = _dot_nt(kb.astype(_BF16), k_bf) * decay * mask_ref[M_STRICT]
        lbf_ref[h, b] = lmat.astype(_BF16)
        tbf_ref[h, b] = (mask_ref[M_EYE] - lmat * mask_ref[M_PAIR]).astype(_BF16)
        intra_ref[h, b] = (_dot_nt(q.astype(_BF16), k_bf) * decay).astype(_BF16)
        egc = jnp.exp(gc_c)
        rhs_ref[h, rb, 0:HEAD_DIM] = (v * beta_c).astype(_BF16)
        rhs_ref[h, rb, HEAD_DIM:2 * HEAD_DIM] = (kb * egc).astype(_BF16)
        qg = (q * egc).astype(_BF16)
        for c in range(per_sub):
            wq_ref[h, b * per_sub + c, CHUNK:2 * CHUNK, :] = qg[c * CHUNK:(c + 1) * CHUNK]
        kdt_ref[h, b] = (k * jnp.exp(gl_c - gc_c)).T.astype(_BF16)

    for lv in range(N_LEVELS):
        for h, b in units:
            p = _dot(lbf_ref[h, b], tbf_ref[h, b]) * mask_ref[M_OFF0 + lv]
            pbf_ref[h, b] = p.astype(_BF16)
        for h, b in units:
            t = tbf_ref[h, b]
            tbf_ref[h, b] = t - _dot(t, pbf_ref[h, b]).astype(_BF16)

    for h, b in units:
        rb = slice(b * SUB, (b + 1) * SUB)
        uw = _dot(tbf_ref[h, b], rhs_ref[h, rb, :])
        u_ref[h, rb, :] = uw[:, :HEAD_DIM]
        w_bf = uw[:, HEAD_DIM:].astype(_BF16)
        for c in range(per_sub):
            wq_ref[h, b * per_sub + c, 0:CHUNK, :] = w_bf[c * CHUNK:(c + 1) * CHUNK]

    lane_chunk = lax.broadcasted_iota(jnp.int32, (1, SUB), 1) // CHUNK
    for n in range(n_chunks):
        b, c = divmod(n, per_sub)
        rs = slice(n * CHUNK, (n + 1) * CHUNK)
        rc = slice(c * CHUNK, (c + 1) * CHUNK)
        for h in range(N_HEADS):
            ws = _dot(wq_ref[h, n], state_ref[h].astype(_BF16))
            vnew_ref[h, b, rc, :] = (u_ref[h, rs, :] - ws[:CHUNK]).astype(_BF16)
            u_ref[h, rs, :] = ws[CHUNK:]
        for h in range(N_HEADS):
            sl = slice(h * HEAD_DIM, (h + 1) * HEAD_DIM)
            vn = vnew_ref[h, b]
            o_n = u_ref[h, rs, :] + _dot(intra_ref[h, b, rc, :], vn)
            kd = jnp.where(lane_chunk == c, kdt_ref[h, b], jnp.zeros((), _BF16))
            gl = scal_ref[n * CHUNK:n * CHUNK + 1, 24 + h:25 + h]
            state_ref[h] = state_ref[h] * jnp.exp(gl) + _dot(kd, vn)
            ms = jnp.mean(o_n * o_n, axis=-1, keepdims=True)
            zn = z_ref[rs, sl].astype(_F32)
            o_ref[rs, sl] = (o_n * lax.rsqrt(ms + EPS) * gain_ref[...] * _silu(zn)).astype(_BF16)


def _delta(qkv, big3, scal3, scalt, gain):
    bsz, seq, _ = qkv.shape
    ng = seq // GROUP
    n_sub = GROUP // SUB
    hm = (N_HEADS, n_sub, SUB, SUB)
    return pl.pallas_call(
        _delta_kernel,
        grid=(bsz, ng),
        in_specs=[pl.BlockSpec((None, GROUP, D_MODEL), lambda b, g: (b, g, 0)),
                  pl.BlockSpec((None, GROUP, D_MODEL), lambda b, g: (b, g, 1)),
                  pl.BlockSpec((None, GROUP, D_MODEL), lambda b, g: (b, g, 2)),
                  pl.BlockSpec((None, GROUP, D_MODEL), lambda b, g: (b, g, CB_BZ)),
                  pl.BlockSpec((None, GROUP, LANES), lambda b, g: (b, g, 0)),
                  pl.BlockSpec((None, 32, GROUP), lambda b, g: (b, 0, g)),
                  pl.BlockSpec((1, HEAD_DIM), lambda b, g: (0, 0))],
        out_specs=pl.BlockSpec((None, GROUP, D_MODEL), lambda b, g: (b, g, 0)),
        out_shape=jax.ShapeDtypeStruct((bsz, seq, D_MODEL), _BF16),
        scratch_shapes=[pltpu.VMEM((N_HEADS, HEAD_DIM, HEAD_DIM), _F32),
                        pltpu.VMEM((N_HEADS, n_sub, SUB, HEAD_DIM), _BF16),
                        pltpu.VMEM((M_OFF0 + N_LEVELS, SUB, SUB), _F32),
                        pltpu.VMEM(hm, _BF16),
                        pltpu.VMEM(hm, _BF16),
                        pltpu.VMEM(hm, _BF16),
                        pltpu.VMEM((N_HEADS, GROUP, 2 * HEAD_DIM), _BF16),
                        pltpu.VMEM(hm, _BF16),
                        pltpu.VMEM((N_HEADS, GROUP, HEAD_DIM), _F32),
                        pltpu.VMEM((N_HEADS, GROUP // CHUNK, 2 * CHUNK, HEAD_DIM), _BF16),
                        pltpu.VMEM((N_HEADS, n_sub, HEAD_DIM, SUB), _BF16)],
        compiler_params=_cparams(("parallel", "arbitrary"), 48),
        name="delta",
    )(qkv, qkv, qkv, big3, scal3, scalt, gain)


def _out_kernel(oa_ref, az_ref, ob_ref, ga_ref, gb_ref, x_ref, mod_ref, gain_ref,
                wa_ref, wb_ref, wo_ref, o_ref):
    za = (oa_ref[...].astype(_F32) * _silu(az_ref[...].astype(_F32))).astype(_BF16)
    ya = _dot(za, wa_ref[...])
    yb = _dot(ob_ref[...], wb_ref[...])
    y = jax.nn.sigmoid(ga_ref[...].astype(_F32)) * ya + jax.nn.sigmoid(gb_ref[...].astype(_F32)) * yb
    out = _dot(y.astype(_BF16), wo_ref[...])
    ms = jnp.mean(out * out, axis=-1, keepdims=True)
    gate = mod_ref[0, :, 2 * D_MODEL:3 * D_MODEL]
    o_ref[...] = x_ref[...] + gate * (out * lax.rsqrt(ms + EPS) * gain_ref[...])


def _out(oa2, big, ob2, x2, mod3, post_gain, wa, wb, wo, seq):
    m = x2.shape[0]
    tm = 512
    per_b = seq // tm
    row = lambda i: (i, 0)
    const = lambda i: (0, 0)
    return pl.pallas_call(
        _out_kernel,
        grid=(m // tm,),
        in_specs=[pl.BlockSpec((tm, D_MODEL), row),
                  pl.BlockSpec((tm, D_MODEL), lambda i: (i, CB_AZ)),
                  pl.BlockSpec((tm, D_MODEL), row),
                  pl.BlockSpec((tm, D_MODEL), lambda i: (i, CB_GA)),
                  pl.BlockSpec((tm, D_MODEL), lambda i: (i, CB_GB)),
                  pl.BlockSpec((tm, D_MODEL), row),
                  pl.BlockSpec((1, 1, 3 * D_MODEL), lambda i: (i // per_b, 0, 0)),
                  pl.BlockSpec((1, D_MODEL), const),
                  pl.BlockSpec((D_MODEL, D_MODEL), const),
                  pl.BlockSpec((D_MODEL, D_MODEL), const),
                  pl.BlockSpec((D_MODEL, D_MODEL), const)],
        out_specs=pl.BlockSpec((tm, D_MODEL), row),
        out_shape=jax.ShapeDtypeStruct((m, D_MODEL), _F32),
        compiler_params=_cparams(("parallel",), 48),
        name="out",
    )(oa2, big, ob2, big, big, x2, mod3, post_gain, wa, wb, wo)


def _pad_lanes(v, start):
    return jnp.zeros((1, LANES), _F32).at[0, start:start + v.shape[0]].set(v)


def _layer(x, c, positions, w_ada, b_ada, pre_gain, post_gain, w_in, ln_gain, ln_bias,
           conv_w, a_log, dt_bias, dn_gain, w_a_out, w_b_out, w_o):
    bsz, seq, d = x.shape
    m = bsz * seq
    topk = min(TOPK_MAX, seq // 4)

    pts = np.cumsum([0, 1024, 1024, 1024, 1024, 512, 64, 8, 1024, 1024, 1024, 1024, 8, 8, 1024, 1024])
    col = lambda i: w_in[:, pts[i]:pts[i + 1]]
    (aq, ak, av, az, iq, ik, iw, bq, bk, bv, bz, bbeta, ba, ga, gb) = [col(i) for i in range(15)]
    w_big = jnp.concatenate([ak, az, bq, bk, bv, bz, ga, gb], axis=1).astype(_BF16)
    w_qvt = jnp.concatenate([aq, av], axis=1).astype(_BF16).T
    w_small = jnp.concatenate(
        [ik, ik, iw, bbeta, ba, jnp.zeros((d, SMALL_W - 152), _F32)], axis=1)
    ws_hi = w_small.astype(_BF16)
    ws_lo = (w_small - ws_hi.astype(_F32)).astype(_BF16)
    iq_hi = iq.astype(_BF16)
    wiq_hi = iq_hi.T
    wiq_lo = (iq - iq_hi.astype(_F32)).astype(_BF16).T

    mod = _ada(c, w_ada, b_ada)
    mod3 = mod.reshape(bsz, 1, 3 * d)
    x2 = x.reshape(m, d)
    big, small, qt, vt, iqt = _proj(x2, mod3, pre_gain.reshape(1, d), w_big, ws_hi, ws_lo,
                                    w_qvt, wiq_hi, wiq_lo, seq)

    rot_a = HEAD_DIM // ROPE_FRACTION
    rot_i = IDX_DIM // ROPE_FRACTION
    invf_a = ROPE_THETA ** (-(jnp.arange(rot_a // 2, dtype=_F32) * 2.0 / rot_a))
    invf_i = ROPE_THETA ** (-(jnp.arange(rot_i // 2, dtype=_F32) * 2.0 / rot_i))
    lng = jnp.concatenate([ln_gain, ln_gain]).reshape(1, LANES)
    lnb = jnp.concatenate([ln_bias, ln_bias]).reshape(1, LANES)
    q_t, k_r, qcat_t, kcat, scal = _prep(big, qt, iqt, small, positions.reshape(1, m),
                                         invf_a.reshape(-1, 1), invf_i.reshape(-1, 1), lng, lnb,
                                         _pad_lanes(a_log, 16), _pad_lanes(dt_bias, 16))

    scal3 = scal.reshape(bsz, seq, LANES)
    scalt = jnp.transpose(scal3[:, :, :32], (0, 2, 1))
    o_a = _dsa(qcat_t, kcat.reshape(bsz, seq, -1), q_t, k_r.reshape(bsz, seq, d),
               vt, scalt[:, 0:IDX_HEADS, :], topk)

    big3 = big.reshape(bsz, seq, N_BIG)
    qkv = _dnprep(big3, conv_w)
    o_b = _delta(qkv, big3, scal3, scalt, dn_gain.reshape(1, HEAD_DIM))

    y = _out(o_a.reshape(m, d), big, o_b.reshape(m, d), x2, mod3, post_gain.reshape(1, d),
             w_a_out.astype(_BF16), w_b_out.astype(_BF16), w_o.astype(_BF16), seq)
    return y.reshape(bsz, seq, d)


def kernel(x, c, positions, w_ada, b_ada, pre_norm_gain, post_norm_gain, w_in, idx_k_ln_gain,
           idx_k_ln_bias, dn_conv_w, dn_a_log, dn_dt_bias, dn_norm_gain, w_a_out, w_b_out, w_o):
    for layer in range(w_ada.shape[0]):
        x = _layer(x, c, positions, w_ada[layer], b_ada[layer], pre_norm_gain[layer],
                   post_norm_gain[layer], w_in[layer], idx_k_ln_gain[layer], idx_k_ln_bias[layer],
                   dn_conv_w[layer], dn_a_log[layer], dn_dt_bias[layer], dn_norm_gain[layer],
                   w_a_out[layer], w_b_out[layer], w_o[layer])
    return x
```

```python
import functools

import numpy as np
import jax
import jax.numpy as jnp
from jax import lax
from jax.experimental import pallas as pl
from jax.experimental.pallas import tpu as pltpu

D_MODEL = 1024
N_HEADS = 8
HEAD_DIM = 128
IDX_HEADS = 8
IDX_DIM = 64
TOPK_MAX = 256
CONV_WIDTH = 4
CHUNK = 64
ROPE_THETA = 500000.0
ROPE_FRACTION = 4
EPS = 1e-6

LANES = 128
MXU_DIM = 256
TQ = MXU_DIM
GROUP = MXU_DIM
SUB = 128
SMALL_W = 256
IQ_W = IDX_HEADS * IDX_DIM
NEG_BIG = -1e30
HEAD_GROUP = 8
LOG2E = 1.4426950408889634
COUNT_LANES = 8
INT_MIN = -2147483648

CB_AK, CB_AZ, CB_BQ, CB_BK, CB_BV, CB_BZ, CB_GA, CB_GB = range(8)
N_BIG = 8 * D_MODEL

_F32 = jnp.float32
_BF16 = jnp.bfloat16


def _cparams(sem, vmem_mb):
    return pltpu.CompilerParams(dimension_semantics=sem, vmem_limit_bytes=vmem_mb * 1024 * 1024)


def _split2(a):
    hi = a.astype(_BF16)
    lo = (a - hi.astype(_F32)).astype(_BF16)
    return hi, lo


def _dot(a, b):
    return jnp.dot(a, b, preferred_element_type=_F32)


def _dot_nt(a, b):
    return lax.dot_general(a, b, (((1,), (1,)), ((), ())), preferred_element_type=_F32)


def _silu(x):
    return x * jax.nn.sigmoid(x)


def _ada_kernel(c_ref, w_ref, b_ref, o_ref):
    c1, c2 = _split2(c_ref[...])
    c3 = (c_ref[...] - c1.astype(_F32) - c2.astype(_F32)).astype(_BF16)
    w = w_ref[...]
    w1, w2 = _split2(w)
    w3 = (w - w1.astype(_F32) - w2.astype(_F32)).astype(_BF16)
    acc = _dot(c1, w3) + _dot(c2, w2) + _dot(c3, w1)
    acc = acc + _dot(c1, w2) + _dot(c2, w1)
    acc = acc + _dot(c1, w1)
    o_ref[...] = acc + b_ref[...]


def _ada(c, w_ada, b_ada):
    bsz = c.shape[0]
    n = w_ada.shape[1]
    tn = 512
    return pl.pallas_call(
        _ada_kernel,
        grid=(n // tn,),
        in_specs=[pl.BlockSpec((bsz, D_MODEL), lambda j: (0, 0)),
                  pl.BlockSpec((D_MODEL, tn), lambda j: (0, j)),
                  pl.BlockSpec((1, tn), lambda j: (0, j))],
        out_specs=pl.BlockSpec((bsz, tn), lambda j: (0, j)),
        out_shape=jax.ShapeDtypeStruct((bsz, n), _F32),
        compiler_params=_cparams(("arbitrary",), 32),
        name="ada",
    )(c, w_ada, b_ada.reshape(1, n))


def _proj_kernel(x_ref, mod_ref, gain_ref, wbig_ref, wsh_ref, wsl_ref, wqvt_ref, wiqh_ref, wiql_ref,
                 big_ref, small_ref, qt_ref, vt_ref, iqt_ref, h_ref):
    j = pl.program_id(1)

    @pl.when(j == 0)
    def _():
        x = x_ref[...]
        ms = jnp.mean(x * x, axis=-1, keepdims=True)
        y = x * lax.rsqrt(ms + EPS) * gain_ref[...]
        shift = mod_ref[0, :, 0:D_MODEL]
        scale = mod_ref[0, :, D_MODEL:2 * D_MODEL]
        h = y * (1.0 + scale) + shift
        hi, lo = _split2(h)
        h_ref[...] = hi
        small_ref[...] = (_dot(hi, wsl_ref[...]) + _dot(lo, wsh_ref[...])) + _dot(hi, wsh_ref[...])
        qv = _dot_nt(wqvt_ref[...], hi).astype(_BF16)
        qt_ref[...] = qv[:D_MODEL]
        vt_ref[...] = qv[D_MODEL:]
        iqt_ref[...] = (_dot_nt(wiql_ref[...], hi) + _dot_nt(wiqh_ref[...], lo)) + _dot_nt(wiqh_ref[...], hi)

    big_ref[...] = _dot(h_ref[...], wbig_ref[...]).astype(_BF16)


def _proj(x2, mod3, pre_gain, w_big, ws_hi, ws_lo, w_qvt, wiq_hi, wiq_lo, seq):
    m = x2.shape[0]
    tm, tn = min(1024, seq), 1024
    per_b = seq // tm
    const = lambda i, j: (0, 0)
    colblk = lambda i, j: (0, i)
    once = pl.Buffered(1)
    return pl.pallas_call(
        _proj_kernel,
        grid=(m // tm, N_BIG // tn),
        in_specs=[pl.BlockSpec((tm, D_MODEL), lambda i, j: (i, 0)),
                  pl.BlockSpec((1, 1, 3 * D_MODEL), lambda i, j: (i // per_b, 0, 0)),
                  pl.BlockSpec((1, D_MODEL), const),
                  pl.BlockSpec((D_MODEL, tn), lambda i, j: (0, j)),
                  pl.BlockSpec((D_MODEL, SMALL_W), const, pipeline_mode=once),
                  pl.BlockSpec((D_MODEL, SMALL_W), const, pipeline_mode=once),
                  pl.BlockSpec((2 * D_MODEL, D_MODEL), const, pipeline_mode=once),
                  pl.BlockSpec((IQ_W, D_MODEL), const, pipeline_mode=once),
                  pl.BlockSpec((IQ_W, D_MODEL), const, pipeline_mode=once)],
        out_specs=[pl.BlockSpec((tm, tn), lambda i, j: (i, j)),
                   pl.BlockSpec((tm, SMALL_W), lambda i, j: (i, 0)),
                   pl.BlockSpec((D_MODEL, tm), colblk),
                   pl.BlockSpec((D_MODEL, tm), colblk),
                   pl.BlockSpec((IQ_W, tm), colblk)],
        out_shape=[jax.ShapeDtypeStruct((m, N_BIG), _BF16),
                   jax.ShapeDtypeStruct((m, SMALL_W), _F32),
                   jax.ShapeDtypeStruct((D_MODEL, m), _BF16),
                   jax.ShapeDtypeStruct((D_MODEL, m), _BF16),
                   jax.ShapeDtypeStruct((IQ_W, m), _F32)],
        scratch_shapes=[pltpu.VMEM((tm, D_MODEL), _BF16)],
        compiler_params=_cparams(("parallel", "arbitrary"), 58),
        name="proj",
    )(x2, mod3, pre_gain, w_big, ws_hi, ws_lo, w_qvt, wiq_hi, wiq_lo)


def _rope(x, c, s, lower, half):
    partner = jnp.where(lower, pltpu.roll(x, LANES - half, 1), pltpu.roll(x, half, 1))
    return x * c + partner * s


def _prep_kernel(ak_ref, qt_ref, iqt_ref, small_ref, pos_ref, fa_ref, fi_ref, lng_ref, lnb_ref,
                 alog_ref, dtb_ref, qto_ref, k_ref, qcat_ref, kcat_ref, scal_ref):
    tr = ak_ref.shape[0]
    ha, hi_ = 16, 8
    posf = pos_ref[...].astype(_F32)
    ang_a = fa_ref[...] * posf
    cos_a, sin_a = jnp.cos(ang_a), jnp.sin(ang_a)
    ang_i = fi_ref[...] * posf
    cos_i, sin_i = jnp.cos(ang_i), jnp.sin(ang_i)

    q_scale = HEAD_DIM ** -0.5 * LOG2E
    for h in range(N_HEADS):
        b = h * HEAD_DIM
        x1 = qt_ref[b:b + ha, :].astype(_F32)
        x2 = qt_ref[b + ha:b + 2 * ha, :].astype(_F32)
        qto_ref[b:b + ha, :] = ((x1 * cos_a - x2 * sin_a) * q_scale).astype(_BF16)
        qto_ref[b + ha:b + 2 * ha, :] = ((x2 * cos_a + x1 * sin_a) * q_scale).astype(_BF16)
        rest = qt_ref[b + 2 * ha:b + HEAD_DIM, :].astype(_F32)
        qto_ref[b + 2 * ha:b + HEAD_DIM, :] = (rest * q_scale).astype(_BF16)

    for h in range(IDX_HEADS):
        b = h * IDX_DIM
        x1 = iqt_ref[b:b + hi_, :]
        x2 = iqt_ref[b + hi_:b + 2 * hi_, :]
        y = jnp.concatenate([x1 * cos_i - x2 * sin_i, x2 * cos_i + x1 * sin_i,
                             iqt_ref[b + 2 * hi_:b + IDX_DIM, :]], axis=0) * (IDX_DIM ** -0.5)
        yh = y.astype(_BF16)
        yl = (y - yh.astype(_F32)).astype(_BF16)
        o = h * MXU_DIM
        qcat_ref[o:o + IDX_DIM, :] = yh
        qcat_ref[o + IDX_DIM:o + 2 * IDX_DIM, :] = yh
        qcat_ref[o + 2 * IDX_DIM:o + 3 * IDX_DIM, :] = yl
        qcat_ref[o + 3 * IDX_DIM:o + 4 * IDX_DIM, :] = yl

    one_a = jnp.ones((LANES - 2 * ha, tr), _F32)
    zero = lambda n: jnp.zeros((n, tr), _F32)
    c_a = jnp.concatenate([cos_a, cos_a, one_a], axis=0).T
    s_a = jnp.concatenate([-sin_a, sin_a, zero(LANES - 2 * ha)], axis=0).T
    one_i = jnp.ones((IDX_DIM - 2 * hi_, tr), _F32)
    c_i = jnp.concatenate([cos_i, cos_i, one_i] * 2, axis=0).T
    s_i = jnp.concatenate([-sin_i, sin_i, zero(IDX_DIM - 2 * hi_)] * 2, axis=0).T
    lane = lax.broadcasted_iota(jnp.int32, (1, LANES), 1)
    lower_a = lane < ha
    lower_i = (lane % IDX_DIM) < hi_

    for h in range(N_HEADS):
        sl = slice(h * HEAD_DIM, (h + 1) * HEAD_DIM)
        xk = ak_ref[:, sl].astype(_F32)
        k_ref[:, sl] = _rope(xk, c_a, s_a, lower_a, ha).astype(_BF16)

    first = lane < IDX_DIM
    kk = small_ref[:, 0:LANES]
    mu = jnp.mean(kk, axis=-1, keepdims=True)
    var = jnp.mean(jnp.square(kk - mu), axis=-1, keepdims=True)
    kn = (kk - mu) * lax.rsqrt(var + EPS) * lng_ref[...] + lnb_ref[...]
    kr = _rope(kn, c_i, s_i, lower_i, hi_)
    khi = kr.astype(_BF16).astype(_F32)
    kpair = jnp.where(first, khi, kr - khi).astype(_BF16)
    kcat_ref[:, 0:LANES] = kpair
    kcat_ref[:, LANES:2 * LANES] = kpair

    g = small_ref[:, LANES:2 * LANES]
    w_s = g * (IDX_HEADS ** -0.5)
    beta = jax.nn.sigmoid(g)
    z = g + dtb_ref[...]
    softplus = jnp.maximum(z, 0.0) + jnp.log1p(jnp.exp(-jnp.abs(z)))
    gg = -jnp.exp(alog_ref[...]) * softplus
    rowc = lax.broadcasted_iota(jnp.int32, (tr, LANES), 0) % CHUNK
    fwd = gg
    rev = gg
    s = 1
    while s < CHUNK:
        fwd = fwd + jnp.where(rowc >= s, pltpu.roll(fwd, s, 0), 0.0)
        rev = rev + jnp.where(rowc < CHUNK - s, pltpu.roll(rev, tr - s, 0), 0.0)
        s *= 2
    glast = fwd + rev - gg
    out = jnp.where(lane < 8, w_s, jnp.where(lane < 16, beta, jnp.where(lane < 24, fwd, 0.0)))
    out = jnp.where((lane >= 24) & (lane < 32), pltpu.roll(glast, 8, 1), out)
    scal_ref[...] = out


def _prep(big, qt, iqt, small, pos_row, fa, fi, lng, lnb, alog, dtb):
    m = big.shape[0]
    tr = 512
    row = lambda i: (i, 0)
    col = lambda i: (0, i)
    const = lambda i: (0, 0)
    return pl.pallas_call(
        _prep_kernel,
        grid=(m // tr,),
        in_specs=[pl.BlockSpec((tr, D_MODEL), lambda i: (i, CB_AK)),
                  pl.BlockSpec((D_MODEL, tr), col),
                  pl.BlockSpec((IQ_W, tr), col),
                  pl.BlockSpec((tr, SMALL_W), row),
                  pl.BlockSpec((1, tr), col),
                  pl.BlockSpec((16, 1), const), pl.BlockSpec((8, 1), const),
                  pl.BlockSpec((1, LANES), const), pl.BlockSpec((1, LANES), const),
                  pl.BlockSpec((1, LANES), const), pl.BlockSpec((1, LANES), const)],
        out_specs=[pl.BlockSpec((D_MODEL, tr), col), pl.BlockSpec((tr, D_MODEL), row),
                   pl.BlockSpec((IDX_HEADS * MXU_DIM, tr), col), pl.BlockSpec((tr, MXU_DIM), row),
                   pl.BlockSpec((tr, LANES), row)],
        out_shape=[jax.ShapeDtypeStruct((D_MODEL, m), _BF16), jax.ShapeDtypeStruct((m, D_MODEL), _BF16),
                   jax.ShapeDtypeStruct((IDX_HEADS * MXU_DIM, m), _BF16),
                   jax.ShapeDtypeStruct((m, MXU_DIM), _BF16),
                   jax.ShapeDtypeStruct((m, LANES), _F32)],
        compiler_params=_cparams(("parallel",), 48),
        name="prep",
    )(big, qt, iqt, small, pos_row, fa, fi, lng, lnb, alog, dtb)


def _tile_loop(nk, body, init):
    def pair(i, c):
        return body(2 * i + 1, body(2 * i, c))
    c = lax.fori_loop(0, nk // 2, pair, init)
    return lax.cond(nk % 2 == 1, lambda c: body(nk - 1, c), lambda c: c, c)


def _dsa_kernel(topk, qcat_ref, kcat_ref, q_ref, k_ref, vt_ref, wt_ref, o_ref,
                keys_ref, hi16_ref, lo16_ref, bias_ref, s_ref, p_ref, thr_ref):
    qi = pl.program_id(1)
    nk = qi + 1
    kt_rows = lax.broadcasted_iota(jnp.int32, (TQ, TQ), 0)
    row_minus_col = kt_rows - lax.broadcasted_iota(jnp.int32, (TQ, TQ), 1)

    def score_tile(kt, carry):
        r0 = pl.multiple_of(kt * TQ, TQ)
        kc = kcat_ref[pl.ds(r0, TQ), :]
        acc = jnp.zeros((TQ, TQ), _F32)
        for h in range(IDX_HEADS):
            lg = _dot(kc, qcat_ref[h * MXU_DIM:(h + 1) * MXU_DIM, :])
            acc = acc + wt_ref[h:h + 1, :] * jnp.maximum(lg, 0.0)
        bits = pltpu.bitcast(acc + 0.0, jnp.int32)
        key = bits ^ ((bits >> 31) & 0x7FFFFFFF)
        key = jnp.where(row_minus_col <= (qi - kt) * TQ, key, INT_MIN)
        keys_ref[kt] = key
        hi16_ref[kt] = (key >> 16).astype(jnp.int16)
        lo16_ref[kt] = (key - 32768).astype(jnp.int16)
        return carry

    _tile_loop(nk, score_tile, 0)

    def search_static(n_tiles):
        def count16(ref, pred_fn):
            accs = [jnp.zeros((16, TQ), jnp.int16) for _ in range(COUNT_LANES)]
            for kt in range(n_tiles):
                m = jnp.where(pred_fn(ref[kt]), jnp.int16(1), jnp.int16(0))
                for r in range(TQ // 16):
                    accs[r % COUNT_LANES] = accs[r % COUNT_LANES] + m[r * 16:(r + 1) * 16]
            acc = functools.reduce(lambda a, b: a + b, accs)
            return jnp.sum(acc.astype(jnp.int32), axis=0, keepdims=True)

        def search16(ref, want, floor_bits=None):
            if floor_bits is None:
                cnt0 = count16(ref, lambda kv: kv >= jnp.int16(0))
                ok0 = cnt0 >= want
                thr = jnp.where(ok0, 0, -32768)
                cnt_thr = jnp.where(ok0, cnt0, n_tiles * TQ)
                n_bits = 15
            else:
                thr = jnp.full((1, TQ), -(1 << floor_bits), jnp.int32)
                cnt_thr = jnp.zeros((1, TQ), jnp.int32)
                n_bits = floor_bits

            def bit_step(it, carry):
                thr, cnt_thr = carry
                cand = thr + lax.shift_left(jnp.int32(1), n_bits - 1 - it)
                cand16 = cand.astype(jnp.int16)
                cnt = count16(ref, lambda kv: kv >= cand16)
                ok = cnt >= want
                return jnp.where(ok, cand, thr), jnp.where(ok, cnt, cnt_thr)

            return lax.fori_loop(0, n_bits, bit_step, (thr, cnt_thr))

        thr_hi, _ = search16(hi16_ref, topk)
        thr_hi16 = thr_hi.astype(jnp.int16)
        n_above = count16(hi16_ref, lambda kv: kv > thr_hi16)
        for kt in range(n_tiles):
            lo16_ref[kt] = jnp.where(hi16_ref[kt] == thr_hi16, lo16_ref[kt], jnp.int16(-32768))
        thr_lo, cnt_lo = search16(lo16_ref, topk - n_above)
        thr_lo16 = thr_lo.astype(jnp.int16)
        thr = thr_hi * 65536 + (thr_lo + 32768)
        tie_lane = jnp.where((n_above + cnt_lo > topk) & (thr > INT_MIN), 1, 0)
        thr_ref[0:1, :] = thr
        thr_ref[1:2, :] = tie_lane

        @pl.when(jnp.max(tie_lane) > 0)
        def _():
            need = topk - (n_above + count16(lo16_ref, lambda kv: kv > thr_lo16))
            for kt in range(n_tiles):
                neg_pos = (-1 - (kt_rows + kt * TQ)).astype(jnp.int16)
                at_thr = jnp.where(hi16_ref[kt] == thr_hi16, neg_pos, jnp.int16(-32768))
                lo16_ref[kt] = jnp.where(lo16_ref[kt] == thr_lo16, at_thr, jnp.int16(-32768))
            thr_pos, _ = search16(lo16_ref, need, floor_bits=pos_bits)
            thr_ref[2:3, :] = jnp.where(thr > INT_MIN, -1 - thr_pos, -1)

    pos_bits = max(1, (k_ref.shape[0] - 1).bit_length())
    assert pos_bits < 15, "positions are searched as negative int16 values"
    for n_tiles in range(1, keys_ref.shape[0] + 1):
        pl.when(nk == n_tiles)(functools.partial(search_static, n_tiles))
    thr = thr_ref[0:1, :]
    tie = jnp.max(thr_ref[1:2, :]) > 0

    @pl.when(jnp.logical_not(tie))
    def _():
        def body(kt, carry):
            kv = keys_ref[kt]
            bias_ref[kt] = jnp.where(kv >= thr_sel, 0.0, NEG_BIG)
            return carry
        thr_sel = jnp.maximum(thr, INT_MIN + 1)
        _tile_loop(nk, body, 0)

    @pl.when(tie)
    def _():
        last_pos = thr_ref[2:3, :]

        def body(kt, carry):
            kv = keys_ref[kt]
            at_thr = jnp.where((kt_rows + kt * TQ) <= last_pos, 0.0, NEG_BIG)
            bias_ref[kt] = jnp.where(kv > thr, 0.0, jnp.where(kv == thr, at_thr, NEG_BIG))
            return carry
        _tile_loop(nk, body, 0)

    def zero_tail(kt, carry):
        r0 = pl.multiple_of(kt * TQ, TQ)
        for i in range(HEAD_GROUP):
            p_ref[i, pl.ds(r0, TQ), :] = jnp.zeros((TQ, TQ), _BF16)
        return carry

    lax.fori_loop(nk, pl.num_programs(1), zero_tail, 0)
    for hg in range(N_HEADS // HEAD_GROUP):
        hs = [hg * HEAD_GROUP + i for i in range(HEAD_GROUP)]
        sls = [slice(h * HEAD_DIM, (h + 1) * HEAD_DIM) for h in hs]
        qhs = [q_ref[sl, :] for sl in sls]

        def s_tile(kt, mxs):
            r0 = pl.multiple_of(kt * TQ, TQ)
            bias = bias_ref[kt]
            out = []
            for i in range(HEAD_GROUP):
                s = _dot(k_ref[pl.ds(r0, TQ), sls[i]], qhs[i]) + bias
                s_ref[i, kt] = s
                out.append(jnp.maximum(mxs[i], jnp.max(s.reshape(TQ // 8, 8, TQ), axis=0)))
            return tuple(out)

        mxs = _tile_loop(nk, s_tile,
                            tuple(jnp.full((8, TQ), NEG_BIG, _F32) for _ in range(HEAD_GROUP)))
        mxs = [jnp.max(m, axis=0, keepdims=True) for m in mxs]

        def p_tile(kt, ls):
            r0 = pl.multiple_of(kt * TQ, TQ)
            out = []
            for i in range(HEAD_GROUP):
                p = jnp.exp2(s_ref[i, kt] - mxs[i])
                out.append(ls[i] + jnp.sum(p.reshape(TQ // 8, 8, TQ), axis=0))
                p_ref[i, pl.ds(r0, TQ), :] = p.astype(_BF16)
            return tuple(out)

        ls = lax.fori_loop(0, nk, p_tile,
                           tuple(jnp.zeros((8, TQ), _F32) for _ in range(HEAD_GROUP)))
        def pv(k_tiles):
            for i in range(HEAD_GROUP):
                l = jnp.sum(ls[i], axis=0, keepdims=True)
                acc = _dot(vt_ref[sls[i], 0:k_tiles * TQ], p_ref[i, 0:k_tiles * TQ, :])
                o_ref[:, sls[i]] = (acc / l).T.astype(_BF16)

        nq = keys_ref.shape[0]
        sizes = sorted({max(1, nq // 4), max(1, nq // 2), nq})
        for lo_t, k_tiles in zip([0] + sizes[:-1], sizes):
            pl.when((nk > lo_t) & (nk <= k_tiles))(functools.partial(pv, k_tiles))


def _dsa(qcat_t, kcat, q_t, k_r, vt, wt, topk):
    bsz, seq, _ = k_r.shape
    nq = seq // TQ
    return pl.pallas_call(
        functools.partial(_dsa_kernel, topk),
        grid=(bsz, nq),
        in_specs=[pl.BlockSpec((IDX_HEADS * MXU_DIM, TQ), lambda b, i: (0, b * nq + i)),
                  pl.BlockSpec((None, seq, MXU_DIM), lambda b, i: (b, 0, 0)),
                  pl.BlockSpec((D_MODEL, TQ), lambda b, i: (0, b * nq + i)),
                  pl.BlockSpec((None, seq, D_MODEL), lambda b, i: (b, 0, 0)),
                  pl.BlockSpec((D_MODEL, seq), lambda b, i: (0, b)),
                  pl.BlockSpec((None, IDX_HEADS, TQ), lambda b, i: (b, 0, i))],
        out_specs=pl.BlockSpec((None, TQ, D_MODEL), lambda b, i: (b, i, 0)),
        out_shape=jax.ShapeDtypeStruct((bsz, seq, D_MODEL), _BF16),
        scratch_shapes=[pltpu.VMEM((nq, TQ, TQ), jnp.int32),
                        pltpu.VMEM((nq, TQ, TQ), jnp.int16),
                        pltpu.VMEM((nq, TQ, TQ), jnp.int16),
                        pltpu.VMEM((nq, TQ, TQ), _F32),
                        pltpu.VMEM((HEAD_GROUP, nq, TQ, TQ), _F32),
                        pltpu.VMEM((HEAD_GROUP, seq, TQ), _BF16),
                        pltpu.VMEM((8, TQ), jnp.int32)],
        compiler_params=_cparams(("parallel", "arbitrary"), 58),
        name="dsa",
    )(qcat_t, kcat, q_t, k_r, vt, wt)


def _dnprep_kernel(x_ref, w_ref, o_ref):
    j = pl.program_id(1)
    seq, cw = x_ref.shape
    rt = 256
    halo = 16
    ri = lax.broadcasted_iota(jnp.int32, (rt, rt), 0)
    ci = lax.broadcasted_iota(jnp.int32, (rt, rt), 1)
    hr = lax.broadcasted_iota(jnp.int32, (halo, halo), 0)
    hc = lax.broadcasted_iota(jnp.int32, (halo, halo), 1)
    shifts = [jnp.where(ri - ci == d, 1.0, 0.0).astype(_BF16) for d in range(1, CONV_WIDTH)]
    carries = [jnp.where(hc - hr == halo - d, 1.0, 0.0).astype(_BF16) for d in range(1, CONV_WIDTH)]
    per_tensor = D_MODEL // cw
    qk_scale = jnp.where(j < per_tensor, HEAD_DIM ** -0.5, 1.0)

    def chunk(r, normalise):
        xb = x_ref[r * rt:(r + 1) * rt, :]
        acc = w_ref[CONV_WIDTH - 1:CONV_WIDTH, :] * xb.astype(_F32)
        for d in range(1, CONV_WIDTH):
            acc = acc + w_ref[CONV_WIDTH - 1 - d:CONV_WIDTH - d, :] * _dot(shifts[d - 1], xb)
        if r > 0:
            prev = x_ref[r * rt - halo:r * rt, :]
            head = acc[:halo]
            for d in range(1, CONV_WIDTH):
                head = head + w_ref[CONV_WIDTH - 1 - d:CONV_WIDTH - d, :] * _dot(carries[d - 1], prev)
            acc = jnp.concatenate([head, acc[halo:]], axis=0)
        y = _silu(acc)
        if not normalise:
            o_ref[r * rt:(r + 1) * rt, :] = y.astype(_BF16)
            return
        for h in range(cw // HEAD_DIM):
            sl = slice(h * HEAD_DIM, (h + 1) * HEAD_DIM)
            yh = y[:, sl]
            nrm = lax.rsqrt(jnp.sum(yh * yh, axis=-1, keepdims=True) + EPS) * qk_scale
            o_ref[r * rt:(r + 1) * rt, sl] = (yh * nrm).astype(_BF16)

    @pl.when(j < 2 * per_tensor)
    def _():
        for r in range(seq // rt):
            chunk(r, True)

    @pl.when(j >= 2 * per_tensor)
    def _():
        for r in range(seq // rt):
            chunk(r, False)


def _dnprep(big3, conv_w):
    bsz, seq, _ = big3.shape
    cw = 512
    nblk = 3 * D_MODEL // cw
    cb = CB_BQ * D_MODEL // cw
    return pl.pallas_call(
        _dnprep_kernel,
        grid=(bsz, nblk),
        in_specs=[pl.BlockSpec((None, seq, cw), lambda b, j: (b, 0, cb + j)),
                  pl.BlockSpec((CONV_WIDTH, cw), lambda b, j: (0, j))],
        out_specs=pl.BlockSpec((None, seq, cw), lambda b, j: (b, 0, j)),
        out_shape=jax.ShapeDtypeStruct((bsz, seq, 3 * D_MODEL), _BF16),
        compiler_params=_cparams(("parallel", "arbitrary"), 48),
        name="dnprep",
    )(big3, conv_w)


N_LEVELS = 5
M_INCL, M_STRICT, M_PAIR, M_EYE, M_OFF0 = 0, 1, 2, 3, 4


def _delta_kernel(q_ref, k_ref, v_ref, z_ref, scal_ref, scalt_ref, gain_ref, o_ref,
                  state_ref, vnew_ref, mask_ref, lbf_ref, tbf_ref, pbf_ref, rhs_ref,
                  intra_ref, u_ref, wq_ref, kdt_ref):
    g_idx = pl.program_id(1)
    n_chunks = GROUP // CHUNK
    n_sub = GROUP // SUB
    per_sub = SUB // CHUNK
    units = [(h, b) for h in range(N_HEADS) for b in range(n_sub)]

    @pl.when(g_idx == 0)
    def _():
        state_ref[...] = jnp.zeros_like(state_ref)
        vnew_ref[...] = jnp.zeros_like(vnew_ref)
        ri = lax.broadcasted_iota(jnp.int32, (SUB, SUB), 0)
        ci = lax.broadcasted_iota(jnp.int32, (SUB, SUB), 1)
        same = (ri // CHUNK) == (ci // CHUNK)
        mask_ref[M_INCL] = jnp.where(same & (ri >= ci), 0.0, -jnp.inf)
        mask_ref[M_STRICT] = jnp.where(same & (ri > ci), 1.0, 0.0)
        mask_ref[M_PAIR] = jnp.where((ri // 2) == (ci // 2), 1.0, 0.0)
        mask_ref[M_EYE] = jnp.where(ri == ci, 1.0, 0.0)
        for lv in range(N_LEVELS):
            blk = 4 << lv
            off = ((ri // blk) == (ci // blk)) & ((ri // (blk // 2)) != (ci // (blk // 2)))
            mask_ref[M_OFF0 + lv] = jnp.where(off, 1.0, 0.0)

    for h, b in units:
        sl = slice(h * HEAD_DIM, (h + 1) * HEAD_DIM)
        rb = slice(b * SUB, (b + 1) * SUB)
        q = q_ref[rb, sl].astype(_F32)
        k_bf = k_ref[rb, sl]
        k = k_bf.astype(_F32)
        v = v_ref[rb, sl].astype(_F32)
        beta_c = scal_ref[rb, 8 + h:9 + h]
        gc_c = scal_ref[rb, 16 + h:17 + h]
        gl_c = scal_ref[rb, 24 + h:25 + h]
        gc_r = scalt_ref[16 + h:17 + h, rb]
        decay = jnp.exp((gc_c - gc_r) + mask_ref[M_INCL])
        kb = k * beta_c
        lmat = _dot_nt(kb.astype(_BF16), k_bf) * decay * mask_ref[M_STRICT]
        lbf_ref[h, b] = lmat.astype(_BF16)
        tbf_ref[h, b] = (mask_ref[M_EYE] - lmat * mask_ref[M_PAIR]).astype(_BF16)
        intra_ref[h, b] = (_dot_nt(q.astype(_BF16), k_bf) * decay).astype(_BF16)
        egc = jnp.exp(gc_c)
        rhs_ref[h, rb, 0:HEAD_DIM] = (v * beta_c).astype(_BF16)
        rhs_ref[h, rb, HEAD_DIM:2 * HEAD_DIM] = (kb * egc).astype(_BF16)
        qg = (q * egc).astype(_BF16)
        for c in range(per_sub):
            wq_ref[h, b * per_sub + c, CHUNK:2 * CHUNK, :] = qg[c * CHUNK:(c + 1) * CHUNK]
        kdt_ref[h, b] = (k * jnp.exp(gl_c - gc_c)).T.astype(_BF16)

    for lv in range(N_LEVELS):
        for h, b in units:
            p = _dot(lbf_ref[h, b], tbf_ref[h, b]) * mask_ref[M_OFF0 + lv]
            pbf_ref[h, b] = p.astype(_BF16)
        for h, b in units:
            t = tbf_ref[h, b]
            tbf_ref[h, b] = t - _dot(t, pbf_ref[h, b]).astype(_BF16)

    for h, b in units:
        rb = slice(b * SUB, (b + 1) * SUB)
        uw = _dot(tbf_ref[h, b], rhs_ref[h, rb, :])
        u_ref[h, rb, :] = uw[:, :HEAD_DIM]
        w_bf = uw[:, HEAD_DIM:].astype(_BF16)
        for c in range(per_sub):
            wq_ref[h, b * per_sub + c, 0:CHUNK, :] = w_bf[c * CHUNK:(c + 1) * CHUNK]

    lane_chunk = lax.broadcasted_iota(jnp.int32, (1, SUB), 1) // CHUNK
    for n in range(n_chunks):
        b, c = divmod(n, per_sub)
        rs = slice(n * CHUNK, (n + 1) * CHUNK)
        rc = slice(c * CHUNK, (c + 1) * CHUNK)
        for h in range(N_HEADS):
            ws = _dot(wq_ref[h, n], state_ref[h].astype(_BF16))
            vnew_ref[h, b, rc, :] = (u_ref[h, rs, :] - ws[:CHUNK]).astype(_BF16)
            u_ref[h, rs, :] = ws[CHUNK:]
        for h in range(N_HEADS):
            sl = slice(h * HEAD_DIM, (h + 1) * HEAD_DIM)
            vn = vnew_ref[h, b]
            o_n = u_ref[h, rs, :] + _dot(intra_ref[h, b, rc, :], vn)
            kd = jnp.where(lane_chunk == c, kdt_ref[h, b], jnp.zeros((), _BF16))
            gl = scal_ref[n * CHUNK:n * CHUNK + 1, 24 + h:25 + h]
            state_ref[h] = state_ref[h] * jnp.exp(gl) + _dot(kd, vn)
            ms = jnp.mean(o_n * o_n, axis=-1, keepdims=True)
            zn = z_ref[rs, sl].astype(_F32)
            o_ref[rs, sl] = (o_n * lax.rsqrt(ms + EPS) * gain_ref[...] * _silu(zn)).astype(_BF16)


def _delta(qkv, big3, scal3, scalt, gain):
    bsz, seq, _ = qkv.shape
    ng = seq // GROUP
    n_sub = GROUP // SUB
    hm = (N_HEADS, n_sub, SUB, SUB)
    return pl.pallas_call(
        _delta_kernel,
        grid=(bsz, ng),
        in_specs=[pl.BlockSpec((None, GROUP, D_MODEL), lambda b, g: (b, g, 0)),
                  pl.BlockSpec((None, GROUP, D_MODEL), lambda b, g: (b, g, 1)),
                  pl.BlockSpec((None, GROUP, D_MODEL), lambda b, g: (b, g, 2)),
                  pl.BlockSpec((None, GROUP, D_MODEL), lambda b, g: (b, g, CB_BZ)),
                  pl.BlockSpec((None, GROUP, LANES), lambda b, g: (b, g, 0)),
                  pl.BlockSpec((None, 32, GROUP), lambda b, g: (b, 0, g)),
                  pl.BlockSpec((1, HEAD_DIM), lambda b, g: (0, 0))],
        out_specs=pl.BlockSpec((None, GROUP, D_MODEL), lambda b, g: (b, g, 0)),
        out_shape=jax.ShapeDtypeStruct((bsz, seq, D_MODEL), _BF16),
        scratch_shapes=[pltpu.VMEM((N_HEADS, HEAD_DIM, HEAD_DIM), _F32),
                        pltpu.VMEM((N_HEADS, n_sub, SUB, HEAD_DIM), _BF16),
                        pltpu.VMEM((M_OFF0 + N_LEVELS, SUB, SUB), _F32),
                        pltpu.VMEM(hm, _BF16),
                        pltpu.VMEM(hm, _BF16),
                        pltpu.VMEM(hm, _BF16),
                        pltpu.VMEM((N_HEADS, GROUP, 2 * HEAD_DIM), _BF16),
                        pltpu.VMEM(hm, _BF16),
                        pltpu.VMEM((N_HEADS, GROUP, HEAD_DIM), _F32),
                        pltpu.VMEM((N_HEADS, GROUP // CHUNK, 2 * CHUNK, HEAD_DIM), _BF16),
                        pltpu.VMEM((N_HEADS, n_sub, HEAD_DIM, SUB), _BF16)],
        compiler_params=_cparams(("parallel", "arbitrary"), 48),
        name="delta",
    )(qkv, qkv, qkv, big3, scal3, scalt, gain)


def _out_kernel(oa_ref, az_ref, ob_ref, ga_ref, gb_ref, x_ref, mod_ref, gain_ref,
                wa_ref, wb_ref, wo_ref, o_ref):
    za = (oa_ref[...].astype(_F32) * _silu(az_ref[...].astype(_F32))).astype(_BF16)
    ya = _dot(za, wa_ref[...])
    yb = _dot(ob_ref[...], wb_ref[...])
    y = jax.nn.sigmoid(ga_ref[...].astype(_F32)) * ya + jax.nn.sigmoid(gb_ref[...].astype(_F32)) * yb
    out = _dot(y.astype(_BF16), wo_ref[...])
    ms = jnp.mean(out * out, axis=-1, keepdims=True)
    gate = mod_ref[0, :, 2 * D_MODEL:3 * D_MODEL]
    o_ref[...] = x_ref[...] + gate * (out * lax.rsqrt(ms + EPS) * gain_ref[...])


def _out(oa2, big, ob2, x2, mod3, post_gain, wa, wb, wo, seq):
    m = x2.shape[0]
    tm = 512
    per_b = seq // tm
    row = lambda i: (i, 0)
    const = lambda i: (0, 0)
    return pl.pallas_call(
        _out_kernel,
        grid=(m // tm,),
        in_specs=[pl.BlockSpec((tm, D_MODEL), row),
                  pl.BlockSpec((tm, D_MODEL), lambda i: (i, CB_AZ)),
                  pl.BlockSpec((tm, D_MODEL), row),
                  pl.BlockSpec((tm, D_MODEL), lambda i: (i, CB_GA)),
                  pl.BlockSpec((tm, D_MODEL), lambda i: (i, CB_GB)),
                  pl.BlockSpec((tm, D_MODEL), row),
                  pl.BlockSpec((1, 1, 3 * D_MODEL), lambda i: (i // per_b, 0, 0)),
                  pl.BlockSpec((1, D_MODEL), const),
                  pl.BlockSpec((D_MODEL, D_MODEL), const),
                  pl.BlockSpec((D_MODEL, D_MODEL), const),
                  pl.BlockSpec((D_MODEL, D_MODEL), const)],
        out_specs=pl.BlockSpec((tm, D_MODEL), row),
        out_shape=jax.ShapeDtypeStruct((m, D_MODEL), _F32),
        compiler_params=_cparams(("parallel",), 48),
        name="out",
    )(oa2, big, ob2, big, big, x2, mod3, post_gain, wa, wb, wo)


def _pad_lanes(v, start):
    return jnp.zeros((1, LANES), _F32).at[0, start:start + v.shape[0]].set(v)


def _layer(x, c, positions, w_ada, b_ada, pre_gain, post_gain, w_in, ln_gain, ln_bias,
           conv_w, a_log, dt_bias, dn_gain, w_a_out, w_b_out, w_o):
    bsz, seq, d = x.shape
    m = bsz * seq
    topk = min(TOPK_MAX, seq // 4)

    pts = np.cumsum([0, 1024, 1024, 1024, 1024, 512, 64, 8, 1024, 1024, 1024, 1024, 8, 8, 1024, 1024])
    col = lambda i: w_in[:, pts[i]:pts[i + 1]]
    (aq, ak, av, az, iq, ik, iw, bq, bk, bv, bz, bbeta, ba, ga, gb) = [col(i) for i in range(15)]
    w_big = jnp.concatenate([ak, az, bq, bk, bv, bz, ga, gb], axis=1).astype(_BF16)
    w_qvt = jnp.concatenate([aq, av], axis=1).astype(_BF16).T
    w_small = jnp.concatenate(
        [ik, ik, iw, bbeta, ba, jnp.zeros((d, SMALL_W - 152), _F32)], axis=1)
    ws_hi = w_small.astype(_BF16)
    ws_lo = (w_small - ws_hi.astype(_F32)).astype(_BF16)
    iq_hi = iq.astype(_BF16)
    wiq_hi = iq_hi.T
    wiq_lo = (iq - iq_hi.astype(_F32)).astype(_BF16).T

    mod = _ada(c, w_ada, b_ada)
    mod3 = mod.reshape(bsz, 1, 3 * d)
    x2 = x.reshape(m, d)
    big, small, qt, vt, iqt = _proj(x2, mod3, pre_gain.reshape(1, d), w_big, ws_hi, ws_lo,
                                    w_qvt, wiq_hi, wiq_lo, seq)

    rot_a = HEAD_DIM // ROPE_FRACTION
    rot_i = IDX_DIM // ROPE_FRACTION
    invf_a = ROPE_THETA ** (-(jnp.arange(rot_a // 2, dtype=_F32) * 2.0 / rot_a))
    invf_i = ROPE_THETA ** (-(jnp.arange(rot_i // 2, dtype=_F32) * 2.0 / rot_i))
    lng = jnp.concatenate([ln_gain, ln_gain]).reshape(1, LANES)
    lnb = jnp.concatenate([ln_bias, ln_bias]).reshape(1, LANES)
    q_t, k_r, qcat_t, kcat, scal = _prep(big, qt, iqt, small, positions.reshape(1, m),
                                         invf_a.reshape(-1, 1), invf_i.reshape(-1, 1), lng, lnb,
                                         _pad_lanes(a_log, 16), _pad_lanes(dt_bias, 16))

    scal3 = scal.reshape(bsz, seq, LANES)
    scalt = jnp.transpose(scal3[:, :, :32], (0, 2, 1))
    o_a = _dsa(qcat_t, kcat.reshape(bsz, seq, -1), q_t, k_r.reshape(bsz, seq, d),
               vt, scalt[:, 0:IDX_HEADS, :], topk)

    big3 = big.reshape(bsz, seq, N_BIG)
    qkv = _dnprep(big3, conv_w)
    o_b = _delta(qkv, big3, scal3, scalt, dn_gain.reshape(1, HEAD_DIM))

    y = _out(o_a.reshape(m, d), big, o_b.reshape(m, d), x2, mod3, post_gain.reshape(1, d),
             w_a_out.astype(_BF16), w_b_out.astype(_BF16), w_o.astype(_BF16), seq)
    return y.reshape(bsz, seq, d)


def kernel(x, c, positions, w_ada, b_ada, pre_norm_gain, post_norm_gain, w_in, idx_k_ln_gain,
           idx_k_ln_bias, dn_conv_w, dn_a_log, dn_dt_bias, dn_norm_gain, w_a_out, w_b_out, w_o):
    for layer in range(w_ada.shape[0]):
        x = _layer(x, c, positions, w_ada[layer], b_ada[layer], pre_norm_gain[layer],
                   post_norm_gain[layer], w_in[layer], idx_k_ln_gain[layer], idx_k_ln_bias[layer],
                   dn_conv_w[layer], dn_a_log[layer], dn_dt_bias[layer], dn_norm_gain[layer],
                   w_a_out[layer], w_b_out[layer], w_o[layer])
    return x
```

```python
import functools

import numpy as np
import jax
import jax.numpy as jnp
from jax import lax
from jax.experimental import pallas as pl
from jax.experimental.pallas import tpu as pltpu

D_MODEL = 1024
N_HEADS = 8
HEAD_DIM = 128
IDX_HEADS = 8
IDX_DIM = 64
TOPK_MAX = 256
CONV_WIDTH = 4
CHUNK = 64
ROPE_THETA = 500000.0
ROPE_FRACTION = 4
EPS = 1e-6

LANES = 128
MXU_DIM = 256
TQ = MXU_DIM
GROUP = MXU_DIM
SUB = 128
SMALL_W = 256
IQ_W = IDX_HEADS * IDX_DIM
NEG_BIG = -1e30
HEAD_GROUP = 8
LOG2E = 1.4426950408889634
COUNT_LANES = 8
INT_MIN = -2147483648

CB_AK, CB_AZ, CB_BQ, CB_BK, CB_BV, CB_BZ, CB_GA, CB_GB = range(8)
N_BIG = 8 * D_MODEL

_F32 = jnp.float32
_BF16 = jnp.bfloat16


def _cparams(sem, vmem_mb):
    return pltpu.CompilerParams(dimension_semantics=sem, vmem_limit_bytes=vmem_mb * 1024 * 1024)


def _split2(a):
    hi = a.astype(_BF16)
    lo = (a - hi.astype(_F32)).astype(_BF16)
    return hi, lo


def _dot(a, b):
    return jnp.dot(a, b, preferred_element_type=_F32)


def _dot_nt(a, b):
    return lax.dot_general(a, b, (((1,), (1,)), ((), ())), preferred_element_type=_F32)


def _silu(x):
    return x * jax.nn.sigmoid(x)


def _ada_kernel(c_ref, w_ref, b_ref, o_ref):
    c1, c2 = _split2(c_ref[...])
    c3 = (c_ref[...] - c1.astype(_F32) - c2.astype(_F32)).astype(_BF16)
    w = w_ref[...]
    w1, w2 = _split2(w)
    w3 = (w - w1.astype(_F32) - w2.astype(_F32)).astype(_BF16)
    acc = _dot(c1, w3) + _dot(c2, w2) + _dot(c3, w1)
    acc = acc + _dot(c1, w2) + _dot(c2, w1)
    acc = acc + _dot(c1, w1)
    o_ref[...] = acc + b_ref[...]


def _ada(c, w_ada, b_ada):
    bsz = c.shape[0]
    n = w_ada.shape[1]
    tn = 512
    return pl.pallas_call(
        _ada_kernel,
        grid=(n // tn,),
        in_specs=[pl.BlockSpec((bsz, D_MODEL), lambda j: (0, 0)),
                  pl.BlockSpec((D_MODEL, tn), lambda j: (0, j)),
                  pl.BlockSpec((1, tn), lambda j: (0, j))],
        out_specs=pl.BlockSpec((bsz, tn), lambda j: (0, j)),
        out_shape=jax.ShapeDtypeStruct((bsz, n), _F32),
        compiler_params=_cparams(("arbitrary",), 32),
        name="ada",
    )(c, w_ada, b_ada.reshape(1, n))


def _proj_kernel(x_ref, mod_ref, gain_ref, wbig_ref, wsh_ref, wsl_ref, wqvt_ref, wiqh_ref, wiql_ref,
                 big_ref, small_ref, qt_ref, vt_ref, iqt_ref, h_ref):
    j = pl.program_id(1)

    @pl.when(j == 0)
    def _():
        x = x_ref[...]
        ms = jnp.mean(x * x, axis=-1, keepdims=True)
        y = x * lax.rsqrt(ms + EPS) * gain_ref[...]
        shift = mod_ref[0, :, 0:D_MODEL]
        scale = mod_ref[0, :, D_MODEL:2 * D_MODEL]
        h = y * (1.0 + scale) + shift
        hi, lo = _split2(h)
        h_ref[...] = hi
        small_ref[...] = (_dot(hi, wsl_ref[...]) + _dot(lo, wsh_ref[...])) + _dot(hi, wsh_ref[...])
        qv = _dot_nt(wqvt_ref[...], hi).astype(_BF16)
        qt_ref[...] = qv[:D_MODEL]
        vt_ref[...] = qv[D_MODEL:]
        iqt_ref[...] = (_dot_nt(wiql_ref[...], hi) + _dot_nt(wiqh_ref[...], lo)) + _dot_nt(wiqh_ref[...], hi)

    big_ref[...] = _dot(h_ref[...], wbig_ref[...]).astype(_BF16)


def _proj(x2, mod3, pre_gain, w_big, ws_hi, ws_lo, w_qvt, wiq_hi, wiq_lo, seq):
    m = x2.shape[0]
    tm, tn = min(1024, seq), 1024
    per_b = seq // tm
    const = lambda i, j: (0, 0)
    colblk = lambda i, j: (0, i)
    once = pl.Buffered(1)
    return pl.pallas_call(
        _proj_kernel,
        grid=(m // tm, N_BIG // tn),
        in_specs=[pl.BlockSpec((tm, D_MODEL), lambda i, j: (i, 0)),
                  pl.BlockSpec((1, 1, 3 * D_MODEL), lambda i, j: (i // per_b, 0, 0)),
                  pl.BlockSpec((1, D_MODEL), const),
                  pl.BlockSpec((D_MODEL, tn), lambda i, j: (0, j)),
                  pl.BlockSpec((D_MODEL, SMALL_W), const, pipeline_mode=once),
                  pl.BlockSpec((D_MODEL, SMALL_W), const, pipeline_mode=once),
                  pl.BlockSpec((2 * D_MODEL, D_MODEL), const, pipeline_mode=once),
                  pl.BlockSpec((IQ_W, D_MODEL), const, pipeline_mode=once),
                  pl.BlockSpec((IQ_W, D_MODEL), const, pipeline_mode=once)],
        out_specs=[pl.BlockSpec((tm, tn), lambda i, j: (i, j)),
                   pl.BlockSpec((tm, SMALL_W), lambda i, j: (i, 0)),
                   pl.BlockSpec((D_MODEL, tm), colblk),
                   pl.BlockSpec((D_MODEL, tm), colblk),
                   pl.BlockSpec((IQ_W, tm), colblk)],
        out_shape=[jax.ShapeDtypeStruct((m, N_BIG), _BF16),
                   jax.ShapeDtypeStruct((m, SMALL_W), _F32),
                   jax.ShapeDtypeStruct((D_MODEL, m), _BF16),
                   jax.ShapeDtypeStruct((D_MODEL, m), _BF16),
                   jax.ShapeDtypeStruct((IQ_W, m), _F32)],
        scratch_shapes=[pltpu.VMEM((tm, D_MODEL), _BF16)],
        compiler_params=_cparams(("parallel", "arbitrary"), 58),
        name="proj",
    )(x2, mod3, pre_gain, w_big, ws_hi, ws_lo, w_qvt, wiq_hi, wiq_lo)


def _rope(x, c, s, lower, half):
    partner = jnp.where(lower, pltpu.roll(x, LANES - half, 1), pltpu.roll(x, half, 1))
    return x * c + partner * s


def _prep_kernel(ak_ref, qt_ref, iqt_ref, small_ref, pos_ref, fa_ref, fi_ref, lng_ref, lnb_ref,
                 alog_ref, dtb_ref, qto_ref, k_ref, qcat_ref, kcat_ref, scal_ref):
    tr = ak_ref.shape[0]
    ha, hi_ = 16, 8
    posf = pos_ref[...].astype(_F32)
    ang_a = fa_ref[...] * posf
    cos_a, sin_a = jnp.cos(ang_a), jnp.sin(ang_a)
    ang_i = fi_ref[...] * posf
    cos_i, sin_i = jnp.cos(ang_i), jnp.sin(ang_i)

    q_scale = HEAD_DIM ** -0.5 * LOG2E
    for h in range(N_HEADS):
        b = h * HEAD_DIM
        x1 = qt_ref[b:b + ha, :].astype(_F32)
        x2 = qt_ref[b + ha:b + 2 * ha, :].astype(_F32)
        qto_ref[b:b + ha, :] = ((x1 * cos_a - x2 * sin_a) * q_scale).astype(_BF16)
        qto_ref[b + ha:b + 2 * ha, :] = ((x2 * cos_a + x1 * sin_a) * q_scale).astype(_BF16)
        rest = qt_ref[b + 2 * ha:b + HEAD_DIM, :].astype(_F32)
        qto_ref[b + 2 * ha:b + HEAD_DIM, :] = (rest * q_scale).astype(_BF16)

    for h in range(IDX_HEADS):
        b = h * IDX_DIM
        x1 = iqt_ref[b:b + hi_, :]
        x2 = iqt_ref[b + hi_:b + 2 * hi_, :]
        y = jnp.concatenate([x1 * cos_i - x2 * sin_i, x2 * cos_i + x1 * sin_i,
                             iqt_ref[b + 2 * hi_:b + IDX_DIM, :]], axis=0) * (IDX_DIM ** -0.5)
        yh = y.astype(_BF16)
        yl = (y - yh.astype(_F32)).astype(_BF16)
        o = h * MXU_DIM
        qcat_ref[o:o + IDX_DIM, :] = yh
        qcat_ref[o + IDX_DIM:o + 2 * IDX_DIM, :] = yh
        qcat_ref[o + 2 * IDX_DIM:o + 3 * IDX_DIM, :] = yl
        qcat_ref[o + 3 * IDX_DIM:o + 4 * IDX_DIM, :] = yl

    one_a = jnp.ones((LANES - 2 * ha, tr), _F32)
    zero = lambda n: jnp.zeros((n, tr), _F32)
    c_a = jnp.concatenate([cos_a, cos_a, one_a], axis=0).T
    s_a = jnp.concatenate([-sin_a, sin_a, zero(LANES - 2 * ha)], axis=0).T
    one_i = jnp.ones((IDX_DIM - 2 * hi_, tr), _F32)
    c_i = jnp.concatenate([cos_i, cos_i, one_i] * 2, axis=0).T
    s_i = jnp.concatenate([-sin_i, sin_i, zero(IDX_DIM - 2 * hi_)] * 2, axis=0).T
    lane = lax.broadcasted_iota(jnp.int32, (1, LANES), 1)
    lower_a = lane < ha
    lower_i = (lane % IDX_DIM) < hi_

    for h in range(N_HEADS):
        sl = slice(h * HEAD_DIM, (h + 1) * HEAD_DIM)
        xk = ak_ref[:, sl].astype(_F32)
        k_ref[:, sl] = _rope(xk, c_a, s_a, lower_a, ha).astype(_BF16)

    first = lane < IDX_DIM
    kk = small_ref[:, 0:LANES]
    mu = jnp.mean(kk, axis=-1, keepdims=True)
    var = jnp.mean(jnp.square(kk - mu), axis=-1, keepdims=True)
    kn = (kk - mu) * lax.rsqrt(var + EPS) * lng_ref[...] + lnb_ref[...]
    kr = _rope(kn, c_i, s_i, lower_i, hi_)
    khi = kr.astype(_BF16).astype(_F32)
    kpair = jnp.where(first, khi, kr - khi).astype(_BF16)
    kcat_ref[:, 0:LANES] = kpair
    kcat_ref[:, LANES:2 * LANES] = kpair

    g = small_ref[:, LANES:2 * LANES]
    w_s = g * (IDX_HEADS ** -0.5)
    beta = jax.nn.sigmoid(g)
    z = g + dtb_ref[...]
    softplus = jnp.maximum(z, 0.0) + jnp.log1p(jnp.exp(-jnp.abs(z)))
    gg = -jnp.exp(alog_ref[...]) * softplus
    rowc = lax.broadcasted_iota(jnp.int32, (tr, LANES), 0) % CHUNK
    fwd = gg
    rev = gg
    s = 1
    while s < CHUNK:
        fwd = fwd + jnp.where(rowc >= s, pltpu.roll(fwd, s, 0), 0.0)
        rev = rev + jnp.where(rowc < CHUNK - s, pltpu.roll(rev, tr - s, 0), 0.0)
        s *= 2
    glast = fwd + rev - gg
    out = jnp.where(lane < 8, w_s, jnp.where(lane < 16, beta, jnp.where(lane < 24, fwd, 0.0)))
    out = jnp.where((lane >= 24) & (lane < 32), pltpu.roll(glast, 8, 1), out)
    scal_ref[...] = out


def _prep(big, qt, iqt, small, pos_row, fa, fi, lng, lnb, alog, dtb):
    m = big.shape[0]
    tr = min(1024, m)
    row = lambda i: (i, 0)
    col = lambda i: (0, i)
    const = lambda i: (0, 0)
    return pl.pallas_call(
        _prep_kernel,
        grid=(m // tr,),
        in_specs=[pl.BlockSpec((tr, D_MODEL), lambda i: (i, CB_AK)),
                  pl.BlockSpec((D_MODEL, tr), col),
                  pl.BlockSpec((IQ_W, tr), col),
                  pl.BlockSpec((tr, SMALL_W), row),
                  pl.BlockSpec((1, tr), col),
                  pl.BlockSpec((16, 1), const), pl.BlockSpec((8, 1), const),
                  pl.BlockSpec((1, LANES), const), pl.BlockSpec((1, LANES), const),
                  pl.BlockSpec((1, LANES), const), pl.BlockSpec((1, LANES), const)],
        out_specs=[pl.BlockSpec((D_MODEL, tr), col), pl.BlockSpec((tr, D_MODEL), row),
                   pl.BlockSpec((IDX_HEADS * MXU_DIM, tr), col), pl.BlockSpec((tr, MXU_DIM), row),
                   pl.BlockSpec((tr, LANES), row)],
        out_shape=[jax.ShapeDtypeStruct((D_MODEL, m), _BF16), jax.ShapeDtypeStruct((m, D_MODEL), _BF16),
                   jax.ShapeDtypeStruct((IDX_HEADS * MXU_DIM, m), _BF16),
                   jax.ShapeDtypeStruct((m, MXU_DIM), _BF16),
                   jax.ShapeDtypeStruct((m, LANES), _F32)],
        compiler_params=_cparams(("parallel",), 48),
        name="prep",
    )(big, qt, iqt, small, pos_row, fa, fi, lng, lnb, alog, dtb)


def _tile_loop(nk, body, init):
    def pair(i, c):
        return body(2 * i + 1, body(2 * i, c))
    c = lax.fori_loop(0, nk // 2, pair, init)
    return lax.cond(nk % 2 == 1, lambda c: body(nk - 1, c), lambda c: c, c)


def _dsa_kernel(topk, qcat_ref, kcat_ref, q_ref, k_ref, vt_ref, wt_ref, o_ref,
                keys_ref, hi16_ref, lo16_ref, bias_ref, s_ref, p_ref, thr_ref):
    qi = pl.program_id(1)
    nk = qi + 1
    kt_rows = lax.broadcasted_iota(jnp.int32, (TQ, TQ), 0)
    row_minus_col = kt_rows - lax.broadcasted_iota(jnp.int32, (TQ, TQ), 1)

    def score_tile(kt, carry):
        r0 = pl.multiple_of(kt * TQ, TQ)
        kc = kcat_ref[pl.ds(r0, TQ), :]
        acc = jnp.zeros((TQ, TQ), _F32)
        for h in range(IDX_HEADS):
            lg = _dot(kc, qcat_ref[h * MXU_DIM:(h + 1) * MXU_DIM, :])
            acc = acc + wt_ref[h:h + 1, :] * jnp.maximum(lg, 0.0)
        bits = pltpu.bitcast(acc + 0.0, jnp.int32)
        key = bits ^ ((bits >> 31) & 0x7FFFFFFF)
        key = jnp.where(row_minus_col <= (qi - kt) * TQ, key, INT_MIN)
        keys_ref[kt] = key
        hi16_ref[kt] = (key >> 16).astype(jnp.int16)
        lo16_ref[kt] = (key - 32768).astype(jnp.int16)
        return carry

    _tile_loop(nk, score_tile, 0)

    def search_static(n_tiles):
        def count16(ref, pred_fn):
            accs = [jnp.zeros((16, TQ), jnp.int16) for _ in range(COUNT_LANES)]
            for kt in range(n_tiles):
                m = jnp.where(pred_fn(ref[kt]), jnp.int16(1), jnp.int16(0))
                for r in range(TQ // 16):
                    accs[r % COUNT_LANES] = accs[r % COUNT_LANES] + m[r * 16:(r + 1) * 16]
            acc = functools.reduce(lambda a, b: a + b, accs)
            return jnp.sum(acc.astype(jnp.int32), axis=0, keepdims=True)

        def search16(ref, want, floor_bits=None):
            if floor_bits is None:
                cnt0 = count16(ref, lambda kv: kv >= jnp.int16(0))
                ok0 = cnt0 >= want
                thr = jnp.where(ok0, 0, -32768)
                cnt_thr = jnp.where(ok0, cnt0, n_tiles * TQ)
                n_bits = 15
            else:
                thr = jnp.full((1, TQ), -(1 << floor_bits), jnp.int32)
                cnt_thr = jnp.zeros((1, TQ), jnp.int32)
                n_bits = floor_bits

            def bit_step(it, carry):
                thr, cnt_thr = carry
                cand = thr + lax.shift_left(jnp.int32(1), n_bits - 1 - it)
                cand16 = cand.astype(jnp.int16)
                cnt = count16(ref, lambda kv: kv >= cand16)
                ok = cnt >= want
                return jnp.where(ok, cand, thr), jnp.where(ok, cnt, cnt_thr)

            return lax.fori_loop(0, n_bits, bit_step, (thr, cnt_thr))

        thr_hi, _ = search16(hi16_ref, topk)
        thr_hi16 = thr_hi.astype(jnp.int16)
        n_above = count16(hi16_ref, lambda kv: kv > thr_hi16)
        for kt in range(n_tiles):
            lo16_ref[kt] = jnp.where(hi16_ref[kt] == thr_hi16, lo16_ref[kt], jnp.int16(-32768))
        thr_lo, cnt_lo = search16(lo16_ref, topk - n_above)
        thr_lo16 = thr_lo.astype(jnp.int16)
        thr = thr_hi * 65536 + (thr_lo + 32768)
        tie_lane = jnp.where((n_above + cnt_lo > topk) & (thr > INT_MIN), 1, 0)
        thr_ref[0:1, :] = thr
        thr_ref[1:2, :] = tie_lane

        @pl.when(jnp.max(tie_lane) > 0)
        def _():
            need = topk - (n_above + count16(lo16_ref, lambda kv: kv > thr_lo16))
            for kt in range(n_tiles):
                neg_pos = (-1 - (kt_rows + kt * TQ)).astype(jnp.int16)
                at_thr = jnp.where(hi16_ref[kt] == thr_hi16, neg_pos, jnp.int16(-32768))
                lo16_ref[kt] = jnp.where(lo16_ref[kt] == thr_lo16, at_thr, jnp.int16(-32768))
            thr_pos, _ = search16(lo16_ref, need, floor_bits=pos_bits)
            thr_ref[2:3, :] = jnp.where(thr > INT_MIN, -1 - thr_pos, -1)

    pos_bits = max(1, (k_ref.shape[0] - 1).bit_length())
    assert pos_bits < 15, "positions are searched as negative int16 values"
    for n_tiles in range(1, keys_ref.shape[0] + 1):
        pl.when(nk == n_tiles)(functools.partial(search_static, n_tiles))
    thr = thr_ref[0:1, :]
    tie = jnp.max(thr_ref[1:2, :]) > 0

    @pl.when(jnp.logical_not(tie))
    def _():
        def body(kt, carry):
            kv = keys_ref[kt]
            bias_ref[kt] = jnp.where(kv >= thr_sel, 0.0, NEG_BIG)
            return carry
        thr_sel = jnp.maximum(thr, INT_MIN + 1)
        _tile_loop(nk, body, 0)

    @pl.when(tie)
    def _():
        last_pos = thr_ref[2:3, :]

        def body(kt, carry):
            kv = keys_ref[kt]
            at_thr = jnp.where((kt_rows + kt * TQ) <= last_pos, 0.0, NEG_BIG)
            bias_ref[kt] = jnp.where(kv > thr, 0.0, jnp.where(kv == thr, at_thr, NEG_BIG))
            return carry
        _tile_loop(nk, body, 0)

    def zero_tail(kt, carry):
        r0 = pl.multiple_of(kt * TQ, TQ)
        for i in range(HEAD_GROUP):
            p_ref[i, pl.ds(r0, TQ), :] = jnp.zeros((TQ, TQ), _BF16)
        return carry

    nq = keys_ref.shape[0]
    sizes = sorted({max(1, nq // 4), max(1, nq // 2), nq})
    k_used = sizes[-1]
    for sz in reversed(sizes[:-1]):
        k_used = jnp.where(nk <= sz, sz, k_used)
    lax.fori_loop(nk, k_used, zero_tail, 0)
    for hg in range(N_HEADS // HEAD_GROUP):
        hs = [hg * HEAD_GROUP + i for i in range(HEAD_GROUP)]
        sls = [slice(h * HEAD_DIM, (h + 1) * HEAD_DIM) for h in hs]
        qhs = [q_ref[sl, :] for sl in sls]

        def s_tile(kt, mxs):
            r0 = pl.multiple_of(kt * TQ, TQ)
            bias = bias_ref[kt]
            out = []
            for i in range(HEAD_GROUP):
                s = _dot(k_ref[pl.ds(r0, TQ), sls[i]], qhs[i]) + bias
                s_ref[i, kt] = s
                out.append(jnp.maximum(mxs[i], jnp.max(s.reshape(TQ // 8, 8, TQ), axis=0)))
            return tuple(out)

        mxs = _tile_loop(nk, s_tile,
                            tuple(jnp.full((8, TQ), NEG_BIG, _F32) for _ in range(HEAD_GROUP)))
        mxs = [jnp.max(m, axis=0, keepdims=True) for m in mxs]

        def p_tile(kt, ls):
            r0 = pl.multiple_of(kt * TQ, TQ)
            out = []
            for i in range(HEAD_GROUP):
                p = jnp.exp2(s_ref[i, kt] - mxs[i])
                out.append(ls[i] + jnp.sum(p.reshape(TQ // 8, 8, TQ), axis=0))
                p_ref[i, pl.ds(r0, TQ), :] = p.astype(_BF16)
            return tuple(out)

        ls = lax.fori_loop(0, nk, p_tile,
                           tuple(jnp.zeros((8, TQ), _F32) for _ in range(HEAD_GROUP)))
        def pv(k_tiles):
            for i in range(HEAD_GROUP):
                l = jnp.sum(ls[i], axis=0, keepdims=True)
                acc = _dot(vt_ref[sls[i], 0:k_tiles * TQ], p_ref[i, 0:k_tiles * TQ, :])
                o_ref[:, sls[i]] = (acc / l).T.astype(_BF16)

        for lo_t, k_tiles in zip([0] + sizes[:-1], sizes):
            pl.when((nk > lo_t) & (nk <= k_tiles))(functools.partial(pv, k_tiles))


def _dsa(qcat_t, kcat, q_t, k_r, vt, wt, topk):
    bsz, seq, _ = k_r.shape
    nq = seq // TQ
    return pl.pallas_call(
        functools.partial(_dsa_kernel, topk),
        grid=(bsz, nq),
        in_specs=[pl.BlockSpec((IDX_HEADS * MXU_DIM, TQ), lambda b, i: (0, b * nq + i)),
                  pl.BlockSpec((None, seq, MXU_DIM), lambda b, i: (b, 0, 0)),
                  pl.BlockSpec((D_MODEL, TQ), lambda b, i: (0, b * nq + i)),
                  pl.BlockSpec((None, seq, D_MODEL), lambda b, i: (b, 0, 0)),
                  pl.BlockSpec((D_MODEL, seq), lambda b, i: (0, b)),
                  pl.BlockSpec((None, IDX_HEADS, TQ), lambda b, i: (b, 0, i))],
        out_specs=pl.BlockSpec((None, TQ, D_MODEL), lambda b, i: (b, i, 0)),
        out_shape=jax.ShapeDtypeStruct((bsz, seq, D_MODEL), _BF16),
        scratch_shapes=[pltpu.VMEM((nq, TQ, TQ), jnp.int32),
                        pltpu.VMEM((nq, TQ, TQ), jnp.int16),
                        pltpu.VMEM((nq, TQ, TQ), jnp.int16),
                        pltpu.VMEM((nq, TQ, TQ), _F32),
                        pltpu.VMEM((HEAD_GROUP, nq, TQ, TQ), _F32),
                        pltpu.VMEM((HEAD_GROUP, seq, TQ), _BF16),
                        pltpu.VMEM((8, TQ), jnp.int32)],
        compiler_params=_cparams(("parallel", "arbitrary"), 58),
        name="dsa",
    )(qcat_t, kcat, q_t, k_r, vt, wt)


def _dnprep_kernel(x_ref, w_ref, o_ref):
    j = pl.program_id(1)
    seq, cw = x_ref.shape
    rt = 256
    halo = 16
    ri = lax.broadcasted_iota(jnp.int32, (rt, rt), 0)
    ci = lax.broadcasted_iota(jnp.int32, (rt, rt), 1)
    hr = lax.broadcasted_iota(jnp.int32, (halo, halo), 0)
    hc = lax.broadcasted_iota(jnp.int32, (halo, halo), 1)
    shifts = [jnp.where(ri - ci == d, 1.0, 0.0).astype(_BF16) for d in range(1, CONV_WIDTH)]
    carries = [jnp.where(hc - hr == halo - d, 1.0, 0.0).astype(_BF16) for d in range(1, CONV_WIDTH)]
    per_tensor = D_MODEL // cw
    qk_scale = jnp.where(j < per_tensor, HEAD_DIM ** -0.5, 1.0)

    def chunk(r, normalise):
        xb = x_ref[r * rt:(r + 1) * rt, :]
        acc = w_ref[CONV_WIDTH - 1:CONV_WIDTH, :] * xb.astype(_F32)
        for d in range(1, CONV_WIDTH):
            acc = acc + w_ref[CONV_WIDTH - 1 - d:CONV_WIDTH - d, :] * _dot(shifts[d - 1], xb)
        if r > 0:
            prev = x_ref[r * rt - halo:r * rt, :]
            head = acc[:halo]
            for d in range(1, CONV_WIDTH):
                head = head + w_ref[CONV_WIDTH - 1 - d:CONV_WIDTH - d, :] * _dot(carries[d - 1], prev)
            acc = jnp.concatenate([head, acc[halo:]], axis=0)
        y = _silu(acc)
        if not normalise:
            o_ref[r * rt:(r + 1) * rt, :] = y.astype(_BF16)
            return
        for h in range(cw // HEAD_DIM):
            sl = slice(h * HEAD_DIM, (h + 1) * HEAD_DIM)
            yh = y[:, sl]
            nrm = lax.rsqrt(jnp.sum(yh * yh, axis=-1, keepdims=True) + EPS) * qk_scale
            o_ref[r * rt:(r + 1) * rt, sl] = (yh * nrm).astype(_BF16)

    @pl.when(j < 2 * per_tensor)
    def _():
        for r in range(seq // rt):
            chunk(r, True)

    @pl.when(j >= 2 * per_tensor)
    def _():
        for r in range(seq // rt):
            chunk(r, False)


def _dnprep(big3, conv_w):
    bsz, seq, _ = big3.shape
    cw = 512
    nblk = 3 * D_MODEL // cw
    cb = CB_BQ * D_MODEL // cw
    return pl.pallas_call(
        _dnprep_kernel,
        grid=(bsz, nblk),
        in_specs=[pl.BlockSpec((None, seq, cw), lambda b, j: (b, 0, cb + j)),
                  pl.BlockSpec((CONV_WIDTH, cw), lambda b, j: (0, j))],
        out_specs=pl.BlockSpec((None, seq, cw), lambda b, j: (b, 0, j)),
        out_shape=jax.ShapeDtypeStruct((bsz, seq, 3 * D_MODEL), _BF16),
        compiler_params=_cparams(("parallel", "arbitrary"), 48),
        name="dnprep",
    )(big3, conv_w)


N_LEVELS = 5
M_INCL, M_STRICT, M_PAIR, M_EYE, M_OFF0 = 0, 1, 2, 3, 4


def _delta_kernel(q_ref, k_ref, v_ref, z_ref, scal_ref, scalt_ref, gain_ref, o_ref,
                  state_ref, vnew_ref, mask_ref, lbf_ref, tbf_ref, pbf_ref, rhs_ref,
                  intra_ref, u_ref, wq_ref, kdt_ref):
    g_idx = pl.program_id(1)
    n_chunks = GROUP // CHUNK
    n_sub = GROUP // SUB
    per_sub = SUB // CHUNK
    units = [(h, b) for h in range(N_HEADS) for b in range(n_sub)]

    @pl.when(g_idx == 0)
    def _():
        state_ref[...] = jnp.zeros_like(state_ref)
        vnew_ref[...] = jnp.zeros_like(vnew_ref)
        ri = lax.broadcasted_iota(jnp.int32, (SUB, SUB), 0)
        ci = lax.broadcasted_iota(jnp.int32, (SUB, SUB), 1)
        same = (ri // CHUNK) == (ci // CHUNK)
        mask_ref[M_INCL] = jnp.where(same & (ri >= ci), 0.0, -jnp.inf)
        mask_ref[M_STRICT] = jnp.where(same & (ri > ci), 1.0, 0.0)
        mask_ref[M_PAIR] = jnp.where((ri // 2) == (ci // 2), 1.0, 0.0)
        mask_ref[M_EYE] = jnp.where(ri == ci, 1.0, 0.0)
        for lv in range(N_LEVELS):
            blk = 4 << lv
            off = ((ri // blk) == (ci // blk)) & ((ri // (blk // 2)) != (ci // (blk // 2)))
            mask_ref[M_OFF0 + lv] = jnp.where(off, 1.0, 0.0)

    for h, b in units:
        sl = slice(h * HEAD_DIM, (h + 1) * HEAD_DIM)
        rb = slice(b * SUB, (b + 1) * SUB)
        q = q_ref[rb, sl].astype(_F32)
        k_bf = k_ref[rb, sl]
        k = k_bf.astype(_F32)
        v = v_ref[rb, sl].astype(_F32)
        beta_c = scal_ref[rb, 8 + h:9 + h]
        gc_c = scal_ref[rb, 16 + h:17 + h]
        gl_c = scal_ref[rb, 24 + h:25 + h]
        gc_r = scalt_ref[16 + h:17 + h, rb]
        decay = jnp.exp((gc_c - gc_r) + mask_ref[M_INCL])
        kb = k * beta_c
        lmat = _dot_nt(kb.astype(_BF16), k_bf) * decay * mask_ref[M_STRICT]
        lbf_ref[h, b] = lmat.astype(_BF16)
        tbf_ref[h, b] = (mask_ref[M_EYE] - lmat * mask_ref[M_PAIR]).astype(_BF16)
        intra_ref[h, b] = (_dot_nt(q.astype(_BF16), k_bf) * decay).astype(_BF16)
        egc = jnp.exp(gc_c)
        rhs_ref[h, rb, 0:HEAD_DIM] = (v * beta_c).astype(_BF16)
        rhs_ref[h, rb, HEAD_DIM:2 * HEAD_DIM] = (kb * egc).astype(_BF16)
        qg = (q * egc).astype(_BF16)
        for c in range(per_sub):
            wq_ref[h, b * per_sub + c, CHUNK:2 * CHUNK, :] = qg[c * CHUNK:(c + 1) * CHUNK]
        kdt_ref[h, b] = (k * jnp.exp(gl_c - gc_c)).T.astype(_BF16)

    for lv in range(N_LEVELS):
        for h, b in units:
            p = _dot(lbf_ref[h, b], tbf_ref[h, b]) * mask_ref[M_OFF0 + lv]
            pbf_ref[h, b] = p.astype(_BF16)
        for h, b in units:
            t = tbf_ref[h, b]
            tbf_ref[h, b] = t - _dot(t, pbf_ref[h, b]).astype(_BF16)

    for h, b in units:
        rb = slice(b * SUB, (b + 1) * SUB)
        uw = _dot(tbf_ref[h, b], rhs_ref[h, rb, :])
        u_ref[h, rb, :] = uw[:, :HEAD_DIM]
        w_bf = uw[:, HEAD_DIM:].astype(_BF16)
        for c in range(per_sub):
            wq_ref[h, b * per_sub + c, 0:CHUNK, :] = w_bf[c * CHUNK:(c + 1) * CHUNK]

    lane_chunk = lax.broadcasted_iota(jnp.int32, (1, SUB), 1) // CHUNK
    for n in range(n_chunks):
        b, c = divmod(n, per_sub)
        rs = slice(n * CHUNK, (n + 1) * CHUNK)
        rc = slice(c * CHUNK, (c + 1) * CHUNK)
        for h in range(N_HEADS):
            ws = _dot(wq_ref[h, n], state_ref[h].astype(_BF16))
            vnew_ref[h, b, rc, :] = (u_ref[h, rs, :] - ws[:CHUNK]).astype(_BF16)
            u_ref[h, rs, :] = ws[CHUNK:]
        for h in range(N_HEADS):
            sl = slice(h * HEAD_DIM, (h + 1) * HEAD_DIM)
            vn = vnew_ref[h, b]
            o_n = u_ref[h, rs, :] + _dot(intra_ref[h, b, rc, :], vn)
            kd = jnp.where(lane_chunk == c, kdt_ref[h, b], jnp.zeros((), _BF16))
            gl = scal_ref[n * CHUNK:n * CHUNK + 1, 24 + h:25 + h]
            state_ref[h] = state_ref[h] * jnp.exp(gl) + _dot(kd, vn)
            ms = jnp.mean(o_n * o_n, axis=-1, keepdims=True)
            zn = z_ref[rs, sl].astype(_F32)
            o_ref[rs, sl] = (o_n * lax.rsqrt(ms + EPS) * gain_ref[...] * _silu(zn)).astype(_BF16)


def _delta(qkv, big3, scal3, scalt, gain):
    bsz, seq, _ = qkv.shape
    ng = seq // GROUP
    n_sub = GROUP // SUB
    hm = (N_HEADS, n_sub, SUB, SUB)
    return pl.pallas_call(
        _delta_kernel,
        grid=(bsz, ng),
        in_specs=[pl.BlockSpec((None, GROUP, D_MODEL), lambda b, g: (b, g, 0)),
                  pl.BlockSpec((None, GROUP, D_MODEL), lambda b, g: (b, g, 1)),
                  pl.BlockSpec((None, GROUP, D_MODEL), lambda b, g: (b, g, 2)),
                  pl.BlockSpec((None, GROUP, D_MODEL), lambda b, g: (b, g, CB_BZ)),
                  pl.BlockSpec((None, GROUP, LANES), lambda b, g: (b, g, 0)),
                  pl.BlockSpec((None, 32, GROUP), lambda b, g: (b, 0, g)),
                  pl.BlockSpec((1, HEAD_DIM), lambda b, g: (0, 0))],
        out_specs=pl.BlockSpec((None, GROUP, D_MODEL), lambda b, g: (b, g, 0)),
        out_shape=jax.ShapeDtypeStruct((bsz, seq, D_MODEL), _BF16),
        scratch_shapes=[pltpu.VMEM((N_HEADS, HEAD_DIM, HEAD_DIM), _F32),
                        pltpu.VMEM((N_HEADS, n_sub, SUB, HEAD_DIM), _BF16),
                        pltpu.VMEM((M_OFF0 + N_LEVELS, SUB, SUB), _F32),
                        pltpu.VMEM(hm, _BF16),
                        pltpu.VMEM(hm, _BF16),
                        pltpu.VMEM(hm, _BF16),
                        pltpu.VMEM((N_HEADS, GROUP, 2 * HEAD_DIM), _BF16),
                        pltpu.VMEM(hm, _BF16),
                        pltpu.VMEM((N_HEADS, GROUP, HEAD_DIM), _F32),
                        pltpu.VMEM((N_HEADS, GROUP // CHUNK, 2 * CHUNK, HEAD_DIM), _BF16),
                        pltpu.VMEM((N_HEADS, n_sub, HEAD_DIM, SUB), _BF16)],
        compiler_params=_cparams(("parallel", "arbitrary"), 48),
        name="delta",
    )(qkv, qkv, qkv, big3, scal3, scalt, gain)


def _out_kernel(oa_ref, az_ref, ob_ref, ga_ref, gb_ref, x_ref, mod_ref, gain_ref,
                wa_ref, wb_ref, wo_ref, o_ref):
    za = (oa_ref[...].astype(_F32) * _silu(az_ref[...].astype(_F32))).astype(_BF16)
    ya = _dot(za, wa_ref[...])
    yb = _dot(ob_ref[...], wb_ref[...])
    y = jax.nn.sigmoid(ga_ref[...].astype(_F32)) * ya + jax.nn.sigmoid(gb_ref[...].astype(_F32)) * yb
    out = _dot(y.astype(_BF16), wo_ref[...])
    ms = jnp.mean(out * out, axis=-1, keepdims=True)
    gate = mod_ref[0, :, 2 * D_MODEL:3 * D_MODEL]
    o_ref[...] = x_ref[...] + gate * (out * lax.rsqrt(ms + EPS) * gain_ref[...])


def _out(oa2, big, ob2, x2, mod3, post_gain, wa, wb, wo, seq):
    m = x2.shape[0]
    tm = min(1024, seq)
    per_b = seq // tm
    row = lambda i: (i, 0)
    const = lambda i: (0, 0)
    return pl.pallas_call(
        _out_kernel,
        grid=(m // tm,),
        in_specs=[pl.BlockSpec((tm, D_MODEL), row),
                  pl.BlockSpec((tm, D_MODEL), lambda i: (i, CB_AZ)),
                  pl.BlockSpec((tm, D_MODEL), row),
                  pl.BlockSpec((tm, D_MODEL), lambda i: (i, CB_GA)),
                  pl.BlockSpec((tm, D_MODEL), lambda i: (i, CB_GB)),
                  pl.BlockSpec((tm, D_MODEL), row),
                  pl.BlockSpec((1, 1, 3 * D_MODEL), lambda i: (i // per_b, 0, 0)),
                  pl.BlockSpec((1, D_MODEL), const),
                  pl.BlockSpec((D_MODEL, D_MODEL), const),
                  pl.BlockSpec((D_MODEL, D_MODEL), const),
                  pl.BlockSpec((D_MODEL, D_MODEL), const)],
        out_specs=pl.BlockSpec((tm, D_MODEL), row),
        out_shape=jax.ShapeDtypeStruct((m, D_MODEL), _F32),
        compiler_params=_cparams(("parallel",), 48),
        name="out",
    )(oa2, big, ob2, big, big, x2, mod3, post_gain, wa, wb, wo)


def _pad_lanes(v, start):
    return jnp.zeros((1, LANES), _F32).at[0, start:start + v.shape[0]].set(v)


def _layer(x, c, positions, w_ada, b_ada, pre_gain, post_gain, w_in, ln_gain, ln_bias,
           conv_w, a_log, dt_bias, dn_gain, w_a_out, w_b_out, w_o):
    bsz, seq, d = x.shape
    m = bsz * seq
    topk = min(TOPK_MAX, seq // 4)

    pts = np.cumsum([0, 1024, 1024, 1024, 1024, 512, 64, 8, 1024, 1024, 1024, 1024, 8, 8, 1024, 1024])
    col = lambda i: w_in[:, pts[i]:pts[i + 1]]
    (aq, ak, av, az, iq, ik, iw, bq, bk, bv, bz, bbeta, ba, ga, gb) = [col(i) for i in range(15)]
    w_big = jnp.concatenate([ak, az, bq, bk, bv, bz, ga, gb], axis=1).astype(_BF16)
    w_qvt = jnp.concatenate([aq, av], axis=1).astype(_BF16).T
    w_small = jnp.concatenate(
        [ik, ik, iw, bbeta, ba, jnp.zeros((d, SMALL_W - 152), _F32)], axis=1)
    ws_hi = w_small.astype(_BF16)
    ws_lo = (w_small - ws_hi.astype(_F32)).astype(_BF16)
    iq_hi = iq.astype(_BF16)
    wiq_hi = iq_hi.T
    wiq_lo = (iq - iq_hi.astype(_F32)).astype(_BF16).T

    mod = _ada(c, w_ada, b_ada)
    mod3 = mod.reshape(bsz, 1, 3 * d)
    x2 = x.reshape(m, d)
    big, small, qt, vt, iqt = _proj(x2, mod3, pre_gain.reshape(1, d), w_big, ws_hi, ws_lo,
                                    w_qvt, wiq_hi, wiq_lo, seq)

    rot_a = HEAD_DIM // ROPE_FRACTION
    rot_i = IDX_DIM // ROPE_FRACTION
    invf_a = ROPE_THETA ** (-(jnp.arange(rot_a // 2, dtype=_F32) * 2.0 / rot_a))
    invf_i = ROPE_THETA ** (-(jnp.arange(rot_i // 2, dtype=_F32) * 2.0 / rot_i))
    lng = jnp.concatenate([ln_gain, ln_gain]).reshape(1, LANES)
    lnb = jnp.concatenate([ln_bias, ln_bias]).reshape(1, LANES)
    q_t, k_r, qcat_t, kcat, scal = _prep(big, qt, iqt, small, positions.reshape(1, m),
                                         invf_a.reshape(-1, 1), invf_i.reshape(-1, 1), lng, lnb,
                                         _pad_lanes(a_log, 16), _pad_lanes(dt_bias, 16))

    scal3 = scal.reshape(bsz, seq, LANES)
    scalt = jnp.transpose(scal3[:, :, :32], (0, 2, 1))
    o_a = _dsa(qcat_t, kcat.reshape(bsz, seq, -1), q_t, k_r.reshape(bsz, seq, d),
               vt, scalt[:, 0:IDX_HEADS, :], topk)

    big3 = big.reshape(bsz, seq, N_BIG)
    qkv = _dnprep(big3, conv_w)
    o_b = _delta(qkv, big3, scal3, scalt, dn_gain.reshape(1, HEAD_DIM))

    y = _out(o_a.reshape(m, d), big, o_b.reshape(m, d), x2, mod3, post_gain.reshape(1, d),
             w_a_out.astype(_BF16), w_b_out.astype(_BF16), w_o.astype(_BF16), seq)
    return y.reshape(bsz, seq, d)


def kernel(x, c, positions, w_ada, b_ada, pre_norm_gain, post_norm_gain, w_in, idx_k_ln_gain,
           idx_k_ln_bias, dn_conv_w, dn_a_log, dn_dt_bias, dn_norm_gain, w_a_out, w_b_out, w_o):
    for layer in range(w_ada.shape[0]):
        x = _layer(x, c, positions, w_ada[layer], b_ada[layer], pre_norm_gain[layer],
                   post_norm_gain[layer], w_in[layer], idx_k_ln_gain[layer], idx_k_ln_bias[layer],
                   dn_conv_w[layer], dn_a_log[layer], dn_dt_bias[layer], dn_norm_gain[layer],
                   w_a_out[layer], w_b_out[layer], w_o[layer])
    return x
```
